```python
import math
import jax, jax.numpy as jnp
from jax import lax
import numpy as np

D_MODEL = 1024
BATCH = 16
SEQ = 2048
DEPTH = 1
DEC_BATCH = 32
DEC_SEQ = 2048
PAST_LEN = 128

CONV_WIDTH = D_MODEL // 2
HYENA_WIDTH = D_MODEL // 2
CONV_KERNEL = 31
SHORT_KERNEL = 3
POS_EMB_DIM = 33
POS_BANDS = (POS_EMB_DIM - 1) // 2
FILTER_FFN = 64
FAST_DECAY_PCT = 0.3
SLOW_DECAY_PCT = 1.5
DECAY_TARGET = 1e-2
N_EXPERTS = 16
EXPERT_D_FF = 1408
CAPACITY_FACTOR = 2
PROJ_COLS = 2 * CONV_WIDTH + 3 * HYENA_WIDTH + 2 * D_MODEL
NORM_EPS = 1e-6
LN_EPS = 1e-5

kernel_name = 'hybrid_conformer_hyena_ec_moe_encoder'


def rmsnorm(x, g):
    xf = x.astype(jnp.float32)
    y = xf * lax.rsqrt(jnp.mean(xf * xf, axis=-1, keepdims=True) + NORM_EPS)
    return (y * g.astype(jnp.float32)).astype(x.dtype)


def layernorm(x, g, b):
    xf = x.astype(jnp.float32)
    mu = jnp.mean(xf, axis=-1, keepdims=True)
    var = jnp.mean(jnp.square(xf - mu), axis=-1, keepdims=True)
    y = (xf - mu) * lax.rsqrt(var + LN_EPS)
    return (y * g.astype(jnp.float32) + b.astype(jnp.float32)).astype(x.dtype)


def depthwise_conv(x, w, b):
    k = w.shape[0]
    y = lax.conv_general_dilated(
        x, w[:, None, :].astype(x.dtype), window_strides=(1,),
        padding=[(k // 2, k // 2)],
        dimension_numbers=('NWC', 'WIO', 'NWC'),
        feature_group_count=x.shape[-1])
    return y + b


def hyena_pos_features(L):
    t = jnp.linspace(0.0, 1.0, L, dtype=jnp.float32)[:, None]
    w = 2.0 * math.pi * jnp.arange(L, dtype=jnp.float32)[:, None] / L
    bands = jnp.linspace(1e-4, POS_BANDS - 1, POS_BANDS, dtype=jnp.float32)[None, :]
    return jnp.concatenate([t, jnp.cos(bands * w), -jnp.sin(bands * w)], axis=-1)


def hyena_filter_spectrum(L, w1, b1, w2, b2, w3, b3, w4, freq):
    z = hyena_pos_features(L)
    f32 = lambda a: a.astype(jnp.float32)
    fr = f32(freq)
    h = jnp.sin(fr * (z @ f32(w1) + f32(b1)))
    h = jnp.sin(fr * (h @ f32(w2) + f32(b2)))
    h = jnp.sin(fr * (h @ f32(w3) + f32(b3)))
    h = (h @ f32(w4)).reshape(L, 2, HYENA_WIDTH)
    t = jnp.linspace(0.0, 1.0, L, dtype=jnp.float32)[:, None]
    max_decay = math.log(DECAY_TARGET) / FAST_DECAY_PCT
    min_decay = math.log(DECAY_TARGET) / SLOW_DECAY_PCT
    deltas = jnp.linspace(min_decay, max_decay, HYENA_WIDTH, dtype=jnp.float32)
    decay = jnp.exp(-t * jnp.abs(deltas)[None, :])
    h = h * decay[:, None, :]
    h = h / jnp.sum(jnp.abs(h), axis=(0, 1), keepdims=True)
    h_fwd, h_bwd = h[:, 0], h[:, 1]
    kern = jnp.concatenate([h_fwd, jnp.zeros((1, HYENA_WIDTH), jnp.float32), h_bwd[:L - 1][::-1]], axis=0)
    return jnp.fft.rfft(kern, n=2 * L, axis=0)


def long_conv(u, k_fft):
    L = u.shape[1]
    u_f = jnp.fft.rfft(u.astype(jnp.float32), n=2 * L, axis=1)
    y = jnp.fft.irfft(u_f * k_fft[None], n=2 * L, axis=1)[:, :L]
    return y.astype(u.dtype)


def expert_choice_moe(h, router_w, w_gate, w_up, w_down):
    B, L, D = h.shape
    T = B * L
    tok = h.reshape(T, D)
    logits = (tok @ router_w).astype(jnp.float32)
    aff = jax.nn.softmax(logits, axis=-1)
    cap = max(1, min(T, CAPACITY_FACTOR * T // N_EXPERTS))
    gate, idx = lax.top_k(aff.T, cap)
    xs = tok[idx]
    hid = jax.nn.silu(jnp.einsum('ecd,edf->ecf', xs, w_gate)) * jnp.einsum('ecd,edf->ecf', xs, w_up)
    ye = jnp.einsum('ecf,efd->ecd', hid, w_down) * gate[..., None].astype(h.dtype)
    out = jnp.zeros_like(tok).at[idx.reshape(-1)].add(ye.reshape(-1, D))
    return out.reshape(B, L, D)


def hybrid_layer(x, norm1_g, w_in, b_in, conv_dw_w, conv_dw_b, conv_ln_g, conv_ln_b,
                 conv_w_out, conv_b_out, hy_short_w, hy_short_b, hy_f_w1, hy_f_b1,
                 hy_f_w2, hy_f_b2, hy_f_w3, hy_f_b3, hy_f_w4, hy_f_freq, hy_skip,
                 hy_w_out, hy_b_out, w_o, norm2_g, router_w, exp_w_gate, exp_w_up, exp_w_down):
    L = x.shape[1]
    h = rmsnorm(x, norm1_g)
    z = jnp.einsum('bld,dc->blc', h, w_in) + b_in
    a_in, hy_in, g_in = jnp.split(z, [2 * CONV_WIDTH, 2 * CONV_WIDTH + 3 * HYENA_WIDTH], axis=-1)
    a = a_in[..., :CONV_WIDTH] * jax.nn.sigmoid(a_in[..., CONV_WIDTH:])
    a = depthwise_conv(a, conv_dw_w, conv_dw_b)
    a = jax.nn.silu(layernorm(a, conv_ln_g, conv_ln_b))
    y_a = a @ conv_w_out + conv_b_out
    u = depthwise_conv(hy_in, hy_short_w, hy_short_b)
    x0, x1, v = jnp.split(u, 3, axis=-1)
    k_fft = hyena_filter_spectrum(L, hy_f_w1, hy_f_b1, hy_f_w2, hy_f_b2, hy_f_w3, hy_f_b3, hy_f_w4, hy_f_freq)
    v = x1 * v
    v = long_conv(v, k_fft) + hy_skip * v
    y_b = (x0 * v) @ hy_w_out + hy_b_out
    g_a, g_b = jnp.split(jax.nn.sigmoid(g_in), 2, axis=-1)
    x = x + (g_a * y_a + g_b * y_b) @ w_o
    x = x + expert_choice_moe(rmsnorm(x, norm2_g), router_w, exp_w_gate, exp_w_up, exp_w_down)
    return x


def setup_inputs(seed: int = 0) -> dict:
    key = jax.random.key(seed)
    ks = iter(jax.random.split(key, 40))
    D, Da, Db, E, F = D_MODEL, CONV_WIDTH, HYENA_WIDTH, N_EXPERTS, EXPERT_D_FF

    def nrm(shape, scale):
        return jax.random.normal(next(ks), shape, jnp.float32) * scale

    return {
        'x_prompt': nrm((BATCH, SEQ, D), 1.0),
        'x_sample': nrm((DEC_BATCH, DEC_SEQ, D), 1.0),
        'norm1_g': 1.0 + nrm((DEPTH, D), 0.02),
        'w_in': nrm((DEPTH, D, PROJ_COLS), D ** -0.5),
        'b_in': nrm((DEPTH, PROJ_COLS), 0.01),
        'conv_dw_w': nrm((DEPTH, CONV_KERNEL, Da), CONV_KERNEL ** -0.5),
        'conv_dw_b': nrm((DEPTH, Da), 0.01),
        'conv_ln_g': 1.0 + nrm((DEPTH, Da), 0.02),
        'conv_ln_b': nrm((DEPTH, Da), 0.01),
        'conv_w_out': nrm((DEPTH, Da, D), Da ** -0.5),
        'conv_b_out': nrm((DEPTH, D), 0.01),
        'hy_short_w': nrm((DEPTH, SHORT_KERNEL, 3 * Db), SHORT_KERNEL ** -0.5),
        'hy_short_b': nrm((DEPTH, 3 * Db), 0.01),
        'hy_f_w1': nrm((DEPTH, POS_EMB_DIM, FILTER_FFN), POS_EMB_DIM ** -0.5),
        'hy_f_b1': nrm((DEPTH, FILTER_FFN), 0.1),
        'hy_f_w2': nrm((DEPTH, FILTER_FFN, FILTER_FFN), FILTER_FFN ** -0.5),
        'hy_f_b2': nrm((DEPTH, FILTER_FFN), 0.1),
        'hy_f_w3': nrm((DEPTH, FILTER_FFN, FILTER_FFN), FILTER_FFN ** -0.5),
        'hy_f_b3': nrm((DEPTH, FILTER_FFN), 0.1),
        'hy_f_w4': nrm((DEPTH, FILTER_FFN, 2 * Db), FILTER_FFN ** -0.5),
        'hy_f_freq': 1.0 + nrm((DEPTH, FILTER_FFN), 0.05),
        'hy_skip': nrm((DEPTH, Db), 1.0),
        'hy_w_out': nrm((DEPTH, Db, D), Db ** -0.5),
        'hy_b_out': nrm((DEPTH, D), 0.01),
        'w_o': nrm((DEPTH, D, D), D ** -0.5),
        'norm2_g': 1.0 + nrm((DEPTH, D), 0.02),
        'router_w': nrm((DEPTH, D, E), D ** -0.5),
        'exp_w_gate': nrm((DEPTH, E, D, F), D ** -0.5),
        'exp_w_up': nrm((DEPTH, E, D, F), D ** -0.5),
        'exp_w_down': nrm((DEPTH, E, F, D), F ** -0.5),
        'norm_f_g': 1.0 + nrm((D,), 0.02),
    }


def reference(x_prompt, x_sample, norm1_g, w_in, b_in, conv_dw_w, conv_dw_b, conv_ln_g,
              conv_ln_b, conv_w_out, conv_b_out, hy_short_w, hy_short_b, hy_f_w1, hy_f_b1,
              hy_f_w2, hy_f_b2, hy_f_w3, hy_f_b3, hy_f_w4, hy_f_freq, hy_skip, hy_w_out,
              hy_b_out, w_o, norm2_g, router_w, exp_w_gate, exp_w_up, exp_w_down, norm_f_g):
    def trunk(x):
        for i in range(DEPTH):
            x = hybrid_layer(
                x, norm1_g[i], w_in[i], b_in[i], conv_dw_w[i], conv_dw_b[i], conv_ln_g[i],
                conv_ln_b[i], conv_w_out[i], conv_b_out[i], hy_short_w[i], hy_short_b[i],
                hy_f_w1[i], hy_f_b1[i], hy_f_w2[i], hy_f_b2[i], hy_f_w3[i], hy_f_b3[i],
                hy_f_w4[i], hy_f_freq[i], hy_skip[i], hy_w_out[i], hy_b_out[i], w_o[i],
                norm2_g[i], router_w[i], exp_w_gate[i], exp_w_up[i], exp_w_down[i])
        return rmsnorm(x, norm_f_g)

    y_prompt = trunk(x_prompt)
    y_sample = trunk(x_sample)
    return (y_prompt, y_sample)
```

```python
import functools
import math

import jax
import jax.numpy as jnp
from jax import lax
from jax.experimental import pallas as pl
from jax.experimental.pallas import tpu as pltpu

F32 = jnp.float32
BF16 = jnp.bfloat16
I32 = jnp.int32

NORM_EPS = 1e-6
LN_EPS = 1e-5
FAST_DECAY_PCT = 0.3
SLOW_DECAY_PCT = 1.5
DECAY_TARGET = 1e-2
CAPACITY_FACTOR = 2

ROUTE_TILE = 256
TOKEN_TILE = 512
CONV_ROWS = 128
CONV_PAD = 16
COMBINE_CHUNK = 512
VMEM_LIMIT = 56 * 1024 * 1024


def _cparams(sem):
    return pltpu.CompilerParams(dimension_semantics=sem, vmem_limit_bytes=VMEM_LIMIT)


def _dot(a, b):
    return jnp.dot(a, b, preferred_element_type=F32)


def _dot_nt(a, b, precision=None):
    return lax.dot_general(a, b, (((1,), (1,)), ((), ())), preferred_element_type=F32, precision=precision)


def _rms(x, g):
    return x * lax.rsqrt(jnp.mean(x * x, axis=-1, keepdims=True) + NORM_EPS) * g


def _filter_kernel(z_ref, w1_ref, b1_ref, w2_ref, b2_ref, w3_ref, b3_ref, w4_ref, fr_ref, dl_ref, sd_ref):
    hp = lax.Precision.HIGHEST
    fr = fr_ref[...]
    z = z_ref[...]
    h = jnp.sin(fr * (jnp.dot(z, w1_ref[...], precision=hp, preferred_element_type=F32) + b1_ref[...]))
    h = jnp.sin(fr * (jnp.dot(h, w2_ref[...], precision=hp, preferred_element_type=F32) + b2_ref[...]))
    h = jnp.sin(fr * (jnp.dot(h, w3_ref[...], precision=hp, preferred_element_type=F32) + b3_ref[...]))
    h4 = jnp.dot(h, w4_ref[...], precision=hp, preferred_element_type=F32)
    L = z.shape[0]
    db = dl_ref.shape[1]
    t = z[:, 0:1]
    decay = jnp.exp(-t * dl_ref[...])
    hf = h4[:, :db] * decay
    hb = h4[:, db:] * decay
    norm = jnp.sum(jnp.abs(hf), axis=0, keepdims=True) + jnp.sum(jnp.abs(hb), axis=0, keepdims=True)
    hf = hf / norm
    hb = hb / norm
    row = lax.broadcasted_iota(I32, (L, 1), 0)
    hbs = jnp.where(row == 0, 0.0, pltpu.roll(hb, 1, axis=0))
    s = hf + hbs
    d = hf - hbs
    s_hi = s.astype(BF16)
    d_hi = d.astype(BF16)
    sd_ref[:, 0 * db:1 * db] = s_hi
    sd_ref[:, 1 * db:2 * db] = (s - s_hi.astype(F32)).astype(BF16)
    sd_ref[:, 2 * db:3 * db] = d_hi
    sd_ref[:, 3 * db:4 * db] = (d - d_hi.astype(F32)).astype(BF16)


def _spec_kernel(wf_ref, sd_ref, ka_ref, kb_ref):
    db = ka_ref.shape[1]
    ks = _dot(wf_ref[...], sd_ref[...])
    ka_ref[...] = ks[:, 0:db] + ks[:, db:2 * db]
    kb_ref[...] = ks[:, 2 * db:3 * db] + ks[:, 3 * db:4 * db]


def _dft_tables(L):
    n2 = 2 * L
    k = jnp.arange(L, dtype=I32)[:, None]
    n = jnp.arange(L, dtype=I32)[None, :]
    ph = ((k * n) % n2).astype(F32) * (2.0 * math.pi / n2)
    c = jnp.cos(ph)
    s = jnp.sin(ph)
    alt = jnp.where(n % 2 == 0, 1.0, -1.0).astype(F32)
    wf_im = jnp.where(k == 0, alt, -s)
    wf = jnp.concatenate([c, wf_im], axis=0).astype(BF16)
    ck = jnp.where(k == 0, 1.0, 2.0) / n2
    wi_re = (c * ck).T
    wi_im = jnp.where(k == 0, alt / n2, -s * ck).T
    wi = jnp.concatenate([wi_re, wi_im], axis=1).astype(BF16)
    return wf, wi


def _filter_spectrum(L, w1, b1, w2, b2, w3, b3, w4, freq, wf):
    pe, ffn = w1.shape
    db = w4.shape[1] // 2
    bands = (pe - 1) // 2
    t = jnp.linspace(0.0, 1.0, L, dtype=F32)[:, None]
    w = 2.0 * math.pi * jnp.arange(L, dtype=F32)[:, None] / L
    bnd = jnp.linspace(1e-4, bands - 1, bands, dtype=F32)[None, :]
    z = jnp.concatenate([t, jnp.cos(bnd * w), -jnp.sin(bnd * w)], axis=-1)
    P = 128
    zp = jnp.zeros((L, P), F32).at[:, :pe].set(z)
    pad2 = lambda a, r, c: jnp.zeros((r, c), F32).at[:a.shape[0], :a.shape[1]].set(a.astype(F32))
    max_decay = math.log(DECAY_TARGET) / FAST_DECAY_PCT
    min_decay = math.log(DECAY_TARGET) / SLOW_DECAY_PCT
    deltas = jnp.abs(jnp.linspace(min_decay, max_decay, db, dtype=F32))[None, :]
    sd = pl.pallas_call(
        _filter_kernel,
        out_shape=jax.ShapeDtypeStruct((L, 4 * db), BF16),
        compiler_params=pltpu.CompilerParams(vmem_limit_bytes=VMEM_LIMIT),
    )(zp, pad2(w1, P, P), pad2(b1[None], 1, P), pad2(w2, P, P), pad2(b2[None], 1, P), pad2(w3, P, P),
      pad2(b3[None], 1, P), pad2(w4, P, 2 * db), pad2(freq[None], 1, P), deltas)
    rb = min(512, L)
    ka, kb = pl.pallas_call(
        _spec_kernel,
        grid=(2 * L // rb,),
        in_specs=[pl.BlockSpec((rb, L), lambda i: (i, 0)), pl.BlockSpec((L, 4 * db), lambda i: (0, 0))],
        out_specs=[pl.BlockSpec((rb, db), lambda i: (i, 0)), pl.BlockSpec((rb, db), lambda i: (i, 0))],
        out_shape=[jax.ShapeDtypeStruct((2 * L, db), F32)] * 2,
        compiler_params=_cparams(("parallel",)),
    )(wf, sd)
    kre = ka[:L]
    kim = jnp.concatenate([ka[L:L + 1], kb[L + 1:]], axis=0)
    return kre, kim


def _inproj_kernel(x_ref, g_ref, w_ref, b_ref, a_ref, hy_ref, gate_ref, *, da, db):
    h = _rms(x_ref[...], g_ref[...]).astype(BF16)
    c0, c1, c2 = da, 2 * da, 2 * da + 3 * db
    za = _dot(h, w_ref[:, 0:c0]) + b_ref[:, 0:c0]
    zg = _dot(h, w_ref[:, c0:c1]) + b_ref[:, c0:c1]
    a_ref[...] = (za * jax.nn.sigmoid(zg)).astype(BF16)
    hy_ref[...] = (_dot(h, w_ref[:, c1:c2]) + b_ref[:, c1:c2]).astype(BF16)
    gate_ref[...] = jax.nn.sigmoid(_dot(h, w_ref[:, c2:]) + b_ref[:, c2:]).astype(BF16)


def _inproj(x, g, w, b, da, db):
    T, D = x.shape
    C = w.shape[1]
    tm = min(TOKEN_TILE, T)
    ng = C - 2 * da - 3 * db
    row = lambda i: (i, 0)
    fix = lambda i: (0, 0)
    return pl.pallas_call(
        functools.partial(_inproj_kernel, da=da, db=db),
        grid=(T // tm,),
        in_specs=[pl.BlockSpec((tm, D), row), pl.BlockSpec((1, D), fix), pl.BlockSpec((D, C), fix),
                  pl.BlockSpec((1, C), fix)],
        out_specs=[pl.BlockSpec((tm, da), row), pl.BlockSpec((tm, 3 * db), row), pl.BlockSpec((tm, ng), row)],
        out_shape=[jax.ShapeDtypeStruct((T, da), BF16), jax.ShapeDtypeStruct((T, 3 * db), BF16),
                   jax.ShapeDtypeStruct((T, ng), BF16)],
        compiler_params=_cparams(("parallel",)),
    )(x, g, w, b)


def _conv_kernel(a_ref, hy_ref, cw_ref, cb_ref, lg_ref, lb_ref, sw_ref, sb_ref,
                 act_ref, x0_ref, vv_ref, pa_ref, ph_ref, sh_ref, cv_ref, *, L, da, db):
    K = cw_ref.shape[0]
    KS = sw_ref.shape[0]
    R = CONV_ROWS
    P = CONV_PAD
    LANES = 128
    zeros_a = jnp.zeros((P, da), F32)
    zeros_h = jnp.zeros((P, 3 * db), F32)
    pa_ref[0:P, :] = zeros_a
    pa_ref[P + L:P + L + P, :] = zeros_a
    ph_ref[0:P, :] = zeros_h
    ph_ref[P + L:P + L + P, :] = zeros_h
    pa_ref[P:P + L, :] = a_ref[0].astype(F32)
    ph_ref[P:P + L, :] = hy_ref[0].astype(F32)

    def chunk(c, carry):
        base = pl.multiple_of(c * R, R)
        for cb in range(da // LANES):
            cols = slice(cb * LANES, (cb + 1) * LANES)
            win = pa_ref[pl.ds(base, R + 2 * P), cols]
            for r in range(8):
                sh_ref[r] = win[r:r + R + 2 * P - 8, :]
            acc = jnp.zeros((R, LANES), F32) + cb_ref[:, cols]
            for k in range(K):
                q, r = divmod(P - K // 2 + k, 8)
                acc = acc + cw_ref[k:k + 1, cols] * sh_ref[r, 8 * q:8 * q + R, :]
            cv_ref[:, cols] = acc
        acc = cv_ref[...]
        mu = jnp.mean(acc, axis=-1, keepdims=True)
        xc = acc - mu
        var = jnp.mean(xc * xc, axis=-1, keepdims=True)
        y = xc * lax.rsqrt(var + LN_EPS) * lg_ref[...] + lb_ref[...]
        act_ref[0, pl.ds(base, R), :] = (y * jax.nn.sigmoid(y)).astype(BF16)

        def short(lo):
            cols = slice(lo, lo + LANES)
            win = ph_ref[pl.ds(base, R + 2 * P), cols]
            u = jnp.zeros((R, LANES), F32) + sb_ref[:, cols]
            for k in range(KS):
                o = P - KS // 2 + k
                u = u + sw_ref[k:k + 1, cols] * win[o:o + R, :]
            return u

        for cb in range(db // LANES):
            lo = cb * LANES
            cols = slice(lo, lo + LANES)
            vv_ref[0, pl.ds(base, R), cols] = (short(db + lo) * short(2 * db + lo)).astype(BF16)
            x0_ref[0, pl.ds(base, R), cols] = short(lo).astype(BF16)
        return carry

    lax.fori_loop(0, L // R, chunk, 0)


def _convs(a, hy, cw, cb, lg, lb, sw, sb):
    B, L, da = a.shape
    db = hy.shape[2] // 3
    bat = lambda b: (b, 0, 0)
    fix = lambda b: (0, 0)
    return pl.pallas_call(
        functools.partial(_conv_kernel, L=L, da=da, db=db),
        grid=(B,),
        in_specs=[pl.BlockSpec((1, L, da), bat), pl.BlockSpec((1, L, 3 * db), bat),
                  pl.BlockSpec(cw.shape, fix), pl.BlockSpec((1, da), fix), pl.BlockSpec((1, da), fix),
                  pl.BlockSpec((1, da), fix), pl.BlockSpec(sw.shape, fix), pl.BlockSpec((1, 3 * db), fix)],
        out_specs=[pl.BlockSpec((1, L, da), bat), pl.BlockSpec((1, L, db), bat), pl.BlockSpec((1, L, db), bat)],
        out_shape=[jax.ShapeDtypeStruct((B, L, da), BF16), jax.ShapeDtypeStruct((B, L, db), BF16),
                   jax.ShapeDtypeStruct((B, L, db), BF16)],
        scratch_shapes=[pltpu.VMEM((L + 2 * CONV_PAD, da), F32), pltpu.VMEM((L + 2 * CONV_PAD, 3 * db), F32),
                        pltpu.VMEM((8, CONV_ROWS + 2 * CONV_PAD - 8, 128), F32), pltpu.VMEM((CONV_ROWS, da), F32)],
        compiler_params=_cparams(("parallel",)),
    )(a, hy, cw, cb, lg, lb, sw, sb)


def _lconv_kernel(v_ref, x0_ref, wfr_ref, wfi_ref, kre_ref, kim_ref, wir_ref, wii_ref, skip_ref,
                  u_ref, acc_ref, *, nfb, fb):
    j = pl.program_id(1)
    v = v_ref[0]
    xre = _dot(wfr_ref[...], v)
    xim = _dot(wfi_ref[...], v)
    kre = kre_ref[...]
    kim = kim_ref[...]
    row0 = (lax.broadcasted_iota(I32, (fb, 1), 0) + j * fb) == 0
    t = xim * kim
    pre = xre * kre - jnp.where(row0, 0.0, t)
    pim = jnp.where(row0, t, xre * kim + xim * kre)
    contrib = _dot(wir_ref[...], pre.astype(BF16)) + _dot(wii_ref[...], pim.astype(BF16))

    @pl.when(j == 0)
    def _():
        acc_ref[...] = contrib

    @pl.when(j > 0)
    def _():
        acc_ref[...] += contrib

    @pl.when(j == nfb - 1)
    def _():
        v2 = acc_ref[...] + skip_ref[...] * v.astype(F32)
        u_ref[0] = (x0_ref[0].astype(F32) * v2).astype(BF16)


def _long_conv(vv, x0, wf, wi, kre, kim, skip):
    B, L, db = vv.shape
    fb = min(512, L)
    nfb = L // fb
    bat = lambda b, j: (b, 0, 0)
    return pl.pallas_call(
        functools.partial(_lconv_kernel, nfb=nfb, fb=fb),
        grid=(B, nfb),
        in_specs=[pl.BlockSpec((1, L, db), bat), pl.BlockSpec((1, L, db), bat),
                  pl.BlockSpec((fb, L), lambda b, j: (j, 0)), pl.BlockSpec((fb, L), lambda b, j: (nfb + j, 0)),
                  pl.BlockSpec((fb, db), lambda b, j: (j, 0)), pl.BlockSpec((fb, db), lambda b, j: (j, 0)),
                  pl.BlockSpec((L, fb), lambda b, j: (0, j)), pl.BlockSpec((L, fb), lambda b, j: (0, nfb + j)),
                  pl.BlockSpec((1, db), lambda b, j: (0, 0))],
        out_specs=pl.BlockSpec((1, L, db), bat),
        out_shape=jax.ShapeDtypeStruct((B, L, db), BF16),
        scratch_shapes=[pltpu.VMEM((L, db), F32)],
        compiler_params=_cparams(("parallel", "arbitrary")),
    )(vv, x0, wf, wf, kre, kim, wi, wi, skip)


def _post_kernel(x_ref, act_ref, u_ref, gate_ref, wa_ref, ba_ref, wb_ref, bb_ref, wo_ref, n2_ref, rwt_ref,
                 x1_ref, tok_ref, aff_ref):
    D = x_ref.shape[1]
    ya = _dot(act_ref[...], wa_ref[...]) + ba_ref[...]
    yb = _dot(u_ref[...], wb_ref[...]) + bb_ref[...]
    g = gate_ref[...].astype(F32)
    m = g[:, :D] * ya + g[:, D:] * yb
    x1 = x_ref[...] + _dot(m.astype(BF16), wo_ref[...])
    x1_ref[...] = x1
    h2 = _rms(x1, n2_ref[...])
    tok_ref[...] = h2
    logits = _dot_nt(rwt_ref[...], h2.astype(BF16))
    ex = jnp.exp(logits - jnp.max(logits, axis=0, keepdims=True))
    aff_ref[...] = ex / jnp.sum(ex, axis=0, keepdims=True)


def _post(x, act, u, gate, wa, ba, wb, bb, wo, n2, rwt):
    T, D = x.shape
    E = rwt.shape[0]
    tm = min(TOKEN_TILE, T)
    row = lambda i: (i, 0)
    fix = lambda i: (0, 0)
    full = lambda a: pl.BlockSpec(a.shape, fix)
    return pl.pallas_call(
        _post_kernel,
        grid=(T // tm,),
        in_specs=[pl.BlockSpec((tm, D), row), pl.BlockSpec((tm, act.shape[1]), row),
                  pl.BlockSpec((tm, u.shape[1]), row), pl.BlockSpec((tm, gate.shape[1]), row),
                  full(wa), full(ba), full(wb), full(bb), full(wo), full(n2), full(rwt)],
        out_specs=[pl.BlockSpec((tm, D), row), pl.BlockSpec((tm, D), row), pl.BlockSpec((E, tm), lambda i: (0, i))],
        out_shape=[jax.ShapeDtypeStruct((T, D), F32), jax.ShapeDtypeStruct((T, D), F32),
                   jax.ShapeDtypeStruct((E, T), F32)],
        compiler_params=_cparams(("parallel",)),
    )(x, act, u, gate, wa, ba, wb, bb, wo, n2, rwt)


def _select_kernel(a_ref, idx_ref, gsel_ref, qloc_ref, ntab_ref, otab_ref, thr_ref, *, cap):
    E, nt, W = a_ref.shape
    nsb = cap // W
    bits = lax.bitcast_convert_type(a_ref[...], I32)

    def bisect(b, cur):
        cand = cur | jnp.left_shift(jnp.int32(1), 30 - b)
        hit = jnp.where(bits >= cand, 1.0, 0.0)
        cnt = jnp.sum(jnp.sum(hit, axis=2, keepdims=True), axis=1, keepdims=True)
        return jnp.where(cnt >= cap, cand, cur)

    thr_ref[...] = lax.fori_loop(0, 31, bisect, jnp.zeros((E, 1, 1), I32))

    r_i = lax.broadcasted_iota(I32, (W, W), 0)
    c_i = lax.broadcasted_iota(I32, (W, W), 1)
    upper = jnp.where(r_i <= c_i, 1.0, 0.0).astype(BF16)
    ones = jnp.ones((W, W), BF16)
    tr = lax.broadcasted_iota(I32, (nt, nt), 0)
    tc = lax.broadcasted_iota(I32, (nt, nt), 1)
    lower = jnp.where(tc < tr, 1.0, 0.0).astype(BF16)
    tile_id = lax.broadcasted_iota(I32, (nt, W), 0).astype(F32)
    lane = lax.broadcasted_iota(I32, (1, W), 1).astype(F32)

    def prefix(maskf):
        mb = maskf.astype(BF16)
        incl = _dot(mb, upper)
        tot = _dot(mb, ones)
        off = _dot(lower, tot.astype(BF16))
        return incl, tot, off

    def expert(e, carry):
        ae = a_ref[e]
        be = lax.bitcast_convert_type(ae, I32)
        th = thr_ref[e]
        gt = be > th
        eq = be == th
        gtf = jnp.where(gt, 1.0, 0.0)
        eqf = jnp.where(eq, 1.0, 0.0)
        need = cap - jnp.sum(jnp.sum(gtf, axis=1, keepdims=True), axis=0, keepdims=True)
        e_incl, _, e_off = prefix(eqf)
        eq_rank = e_off + e_incl - eqf
        sel = gt | (eq & (eq_rank < need))
        self_ = jnp.where(sel, 1.0, 0.0)
        incl, tot, off = prefix(self_)
        gsel_ref[e] = jnp.where(sel, ae, 0.0)
        qloc_ref[e] = incl - 1.0
        ntab_ref[e] = tot[:, 0:128]
        otab_ref[e] = off[:, 0:128]
        incl_t = incl.T.astype(BF16)
        off_hi = off + tot

        def slots(c, carry2):
            s_row = lane + jnp.asarray(c * W, F32)
            oh = jnp.where((off <= s_row) & (s_row < off_hi), 1.0, 0.0)
            tile = jnp.sum(oh * tile_id, axis=0, keepdims=True)
            rank = s_row - jnp.sum(oh * off, axis=0, keepdims=True)
            rt = _dot(incl_t, oh.astype(BF16))
            pos = jnp.sum(jnp.where(rt <= rank, 1.0, 0.0), axis=0, keepdims=True)
            idx_ref[e, pl.ds(c, 1), :] = (tile * W + pos).astype(I32)
            return carry2

        lax.fori_loop(0, nsb, slots, 0)
        return carry

    lax.fori_loop(0, E, expert, 0)


def _select(aff3, cap):
    E, nt, W = aff3.shape
    return pl.pallas_call(
        functools.partial(_select_kernel, cap=cap),
        out_shape=[jax.ShapeDtypeStruct((E, cap // W, W), I32), jax.ShapeDtypeStruct((E, nt, W), F32),
                   jax.ShapeDtypeStruct((E, nt, W), F32), jax.ShapeDtypeStruct((E, nt, 128), F32),
                   jax.ShapeDtypeStruct((E, nt, 128), F32)],
        scratch_shapes=[pltpu.VMEM((E, 1, 1), I32)],
        compiler_params=pltpu.CompilerParams(vmem_limit_bytes=VMEM_LIMIT),
    )(aff3)


def _ffn_kernel(idxc_ref, idxn_ref, tok_hbm, wg_ref, wu_ref, wd_ref, ye_ref, buf, sem, *, nsb, nblk):
    R = ROUTE_TILE
    k = pl.program_id(0) * nsb + pl.program_id(1)
    slot = k % 2

    def issue(idx_ref, dst_slot):
        def body(r, carry):
            t = idx_ref[0, 0, r]
            pltpu.make_async_copy(tok_hbm.at[pl.ds(t, 1)], buf.at[dst_slot, pl.ds(r, 1)], sem.at[dst_slot]).start()
            return carry
        lax.fori_loop(0, R, body, 0, unroll=8)

    @pl.when(k == 0)
    def _():
        issue(idxc_ref, 0)

    @pl.when(k + 1 < nblk)
    def _():
        issue(idxn_ref, 1 - slot)

    pltpu.make_async_copy(tok_hbm.at[pl.ds(0, R)], buf.at[slot], sem.at[slot]).wait()
    xs = buf[slot].astype(BF16)
    g = _dot(xs, wg_ref[0])
    u = _dot(xs, wu_ref[0])
    hid = (g * jax.nn.sigmoid(g) * u).astype(BF16)
    ye_ref[...] = _dot(hid, wd_ref[0])


def _expert_ffn(idx, tok, wg, wu, wd):
    E, nsb, R = idx.shape
    T, D = tok.shape
    F = wg.shape[2]
    nblk = E * nsb
    idx3 = idx.reshape(nblk, 1, R)
    smem = lambda f: pl.BlockSpec((1, 1, R), f, memory_space=pltpu.SMEM)
    return pl.pallas_call(
        functools.partial(_ffn_kernel, nsb=nsb, nblk=nblk),
        grid=(E, nsb),
        in_specs=[smem(lambda e, s: (e * nsb + s, 0, 0)),
                  smem(lambda e, s: (jnp.minimum(e * nsb + s + 1, nblk - 1), 0, 0)),
                  pl.BlockSpec(memory_space=pl.ANY),
                  pl.BlockSpec((1, D, F), lambda e, s: (e, 0, 0)), pl.BlockSpec((1, D, F), lambda e, s: (e, 0, 0)),
                  pl.BlockSpec((1, F, D), lambda e, s: (e, 0, 0))],
        out_specs=pl.BlockSpec((R, D), lambda e, s: (e * nsb + s, 0)),
        out_shape=jax.ShapeDtypeStruct((nblk * R, D), F32),
        scratch_shapes=[pltpu.VMEM((2, R, D), F32), pltpu.SemaphoreType.DMA((2,))],
        compiler_params=_cparams(("arbitrary", "arbitrary")),
    )(idx3, idx3, tok, wg, wu, wd)


_SEG_SIZES = tuple(ROUTE_TILE >> b for b in range(ROUTE_TILE.bit_length() - 3))
_COMBINE_ROWS_PER_EXPERT = ROUTE_TILE + 16


def _combine_kernel(ntab, otab, x1_ref, gsel_ref, qloc_ref, ye_hbm, g_ref, y_ref, buf, sem, *, nt, cap, E):
    W = ROUTE_TILE
    CH = COMBINE_CHUNK
    D = x1_ref.shape[1]
    i = pl.program_id(0)

    def segment_copies(action):
        cum = jnp.int32(0)
        starts, lens = [], []
        for e in range(E):
            n = ntab[e * nt + i]
            src = otab[e * nt + i] + e * cap
            lead = src % 8
            src8 = src - lead
            len8 = ((n + lead + 7) // 8) * 8
            len8 = jnp.where(n > 0, len8, 0)
            starts.append(cum + lead)
            lens.append(n)
            done = jnp.int32(0)
            for size in _SEG_SIZES:
                take = (len8 & size) != 0

                @pl.when(take)
                def _(src8=src8, dst=cum, done=done, size=size):
                    cp = pltpu.make_async_copy(ye_hbm.at[pl.ds(pl.multiple_of(src8 + done, 8), size)],
                                               buf.at[pl.ds(pl.multiple_of(dst + done, 8), size)], sem.at[0])
                    action(cp)

                done = done + jnp.where(take, size, 0)
            cum = cum + len8
        return starts, lens, cum

    segment_copies(lambda cp: cp.start())
    starts, lens, m = segment_copies(lambda cp: cp.wait())

    erow = lax.broadcasted_iota(I32, (E, 1), 0)
    cume = jnp.zeros((E, 1), F32)
    for e in range(E):
        cume = jnp.where(erow == e, starts[e].astype(F32), cume)
    qg = qloc_ref[0] + cume
    eye = jnp.where(lax.broadcasted_iota(I32, (W, W), 0) == lax.broadcasted_iota(I32, (W, W), 1), 1.0, 0.0)
    hp = lax.Precision.HIGHEST
    qg_t = _dot_nt(eye, qg, precision=hp)
    gs_t = _dot_nt(eye, gsel_ref[0], precision=hp)

    def chunk(c, acc):
        base = pl.multiple_of(c * CH, CH)
        rid = base + lax.broadcasted_iota(I32, (CH, 1), 0)
        used = rid < 0
        for e in range(E):
            used = used | ((rid >= starts[e]) & (rid < starts[e] + lens[e]))
        rows = jnp.where(used, buf[pl.ds(base, CH), :], 0.0).astype(BF16)
        qrow = (base + lax.broadcasted_iota(I32, (1, CH), 1)).astype(F32)
        s = jnp.zeros((W, CH), F32)
        for e in range(E):
            s = s + jnp.where(qg_t[:, e:e + 1] == qrow, gs_t[:, e:e + 1], 0.0)
        return acc + _dot(s.astype(BF16), rows)

    moe = lax.fori_loop(0, (m + CH - 1) // CH, chunk, jnp.zeros((W, D), F32))
    y_ref[...] = _rms(x1_ref[...] + moe, g_ref[...])


def _combine(ntab, otab, x1, gsel_t, qloc_t, ye, g, cap):
    T, D = x1.shape
    nt, E, W = gsel_t.shape
    grid_spec = pltpu.PrefetchScalarGridSpec(
        num_scalar_prefetch=2,
        grid=(nt,),
        in_specs=[pl.BlockSpec((W, D), lambda i, *_: (i, 0)),
                  pl.BlockSpec((1, E, W), lambda i, *_: (i, 0, 0)),
                  pl.BlockSpec((1, E, W), lambda i, *_: (i, 0, 0)),
                  pl.BlockSpec(memory_space=pl.ANY),
                  pl.BlockSpec((1, D), lambda i, *_: (0, 0))],
        out_specs=pl.BlockSpec((W, D), lambda i, *_: (i, 0)),
        scratch_shapes=[pltpu.VMEM((pl.cdiv(E * _COMBINE_ROWS_PER_EXPERT, COMBINE_CHUNK) * COMBINE_CHUNK, D), F32),
                        pltpu.SemaphoreType.DMA((1,))],
    )
    return pl.pallas_call(
        functools.partial(_combine_kernel, nt=nt, cap=cap, E=E),
        grid_spec=grid_spec,
        out_shape=jax.ShapeDtypeStruct((T, D), F32),
        compiler_params=_cparams(("arbitrary",)),
    )(ntab, otab, x1, gsel_t, qloc_t, ye, g)


def _layer(x, p, spec_cache):
    B, L, D = x.shape
    T = B * L
    da = p["conv_dw_w"].shape[1]
    db = p["hy_skip"].shape[1]
    E = p["router_wt"].shape[0]
    W = ROUTE_TILE
    cap = max(1, min(T, CAPACITY_FACTOR * T // E))
    assert T % TOKEN_TILE == 0 and cap % W == 0 and L % CONV_ROWS == 0

    if L not in spec_cache:
        wf, wi = _dft_tables(L)
        kre, kim = _filter_spectrum(L, p["hy_f_w1"], p["hy_f_b1"], p["hy_f_w2"], p["hy_f_b2"], p["hy_f_w3"],
                                    p["hy_f_b3"], p["hy_f_w4"], p["hy_f_freq"], wf)
        spec_cache[L] = (wf, wi, kre, kim)
    wf, wi, kre, kim = spec_cache[L]

    x2 = x.reshape(T, D)
    a, hy, gate = _inproj(x2, p["norm1_g"], p["w_in"], p["b_in"], da, db)
    act, x0, vv = _convs(a.reshape(B, L, da), hy.reshape(B, L, 3 * db), p["conv_dw_w"], p["conv_dw_b"],
                         p["conv_ln_g"], p["conv_ln_b"], p["hy_short_w"], p["hy_short_b"])
    u = _long_conv(vv, x0, wf, wi, kre, kim, p["hy_skip"])
    x1, tok, aff = _post(x2, act.reshape(T, da), u.reshape(T, db), gate, p["conv_w_out"], p["conv_b_out"],
                         p["hy_w_out"], p["hy_b_out"], p["w_o"], p["norm2_g"], p["router_wt"])
    nt = T // W
    idx, gsel, qloc, ntab, otab = _select(aff.reshape(E, nt, W), cap)
    ye = _expert_ffn(idx, tok, p["exp_w_gate"], p["exp_w_up"], p["exp_w_down"])
    ntab_i = ntab[:, :, 0].astype(I32).reshape(E * nt)
    otab_i = otab[:, :, 0].astype(I32).reshape(E * nt)
    y = _combine(ntab_i, otab_i, x1, gsel.transpose(1, 0, 2), qloc.transpose(1, 0, 2), ye, p["norm_f_g"], cap)
    return y.reshape(B, L, D)


def kernel(x_prompt, x_sample, norm1_g, w_in, b_in, conv_dw_w, conv_dw_b, conv_ln_g, conv_ln_b, conv_w_out, conv_b_out, hy_short_w, hy_short_b, hy_f_w1, hy_f_b1, hy_f_w2, hy_f_b2, hy_f_w3, hy_f_b3, hy_f_w4, hy_f_freq, hy_skip, hy_w_out, hy_b_out, w_o, norm2_g, router_w, exp_w_gate, exp_w_up, exp_w_down, norm_f_g):
    depth = w_in.shape[0]
    assert depth == 1
    i = 0
    row = lambda a: a[i][None].astype(F32)
    p = {
        "norm1_g": row(norm1_g), "w_in": w_in[i].astype(BF16), "b_in": row(b_in),
        "conv_dw_w": conv_dw_w[i].astype(F32), "conv_dw_b": row(conv_dw_b),
        "conv_ln_g": row(conv_ln_g), "conv_ln_b": row(conv_ln_b),
        "conv_w_out": conv_w_out[i].astype(BF16), "conv_b_out": row(conv_b_out),
        "hy_short_w": hy_short_w[i].astype(F32), "hy_short_b": row(hy_short_b),
        "hy_f_w1": hy_f_w1[i], "hy_f_b1": hy_f_b1[i], "hy_f_w2": hy_f_w2[i], "hy_f_b2": hy_f_b2[i],
        "hy_f_w3": hy_f_w3[i], "hy_f_b3": hy_f_b3[i], "hy_f_w4": hy_f_w4[i], "hy_f_freq": hy_f_freq[i],
        "hy_skip": row(hy_skip), "hy_w_out": hy_w_out[i].astype(BF16), "hy_b_out": row(hy_b_out),
        "w_o": w_o[i].astype(BF16), "norm2_g": row(norm2_g), "router_wt": router_w[i].T.astype(BF16),
        "exp_w_gate": exp_w_gate[i].astype(BF16), "exp_w_up": exp_w_up[i].astype(BF16),
        "exp_w_down": exp_w_down[i].astype(BF16), "norm_f_g": norm_f_g[None].astype(F32),
    }
    spec_cache = {}
    return (_layer(x_prompt, p, spec_cache), _layer(x_sample, p, spec_cache))
```

```python
import functools
import math

import jax
import jax.numpy as jnp
from jax import lax
from jax.experimental import pallas as pl
from jax.experimental.pallas import tpu as pltpu
from jax.experimental.pallas import tpu_sc as plsc

F32 = jnp.float32
BF16 = jnp.bfloat16
I32 = jnp.int32

NORM_EPS = 1e-6
LN_EPS = 1e-5
FAST_DECAY_PCT = 0.3
SLOW_DECAY_PCT = 1.5
DECAY_TARGET = 1e-2
CAPACITY_FACTOR = 2

ROUTE_TILE = 256
TOKEN_TILE = 512
CONV_ROWS = 128
CONV_PAD = 16
COMBINE_CHUNK = 512
VMEM_LIMIT = 56 * 1024 * 1024


def _cparams(sem):
    return pltpu.CompilerParams(dimension_semantics=sem, vmem_limit_bytes=VMEM_LIMIT)


def _dot(a, b):
    return jnp.dot(a, b, preferred_element_type=F32)


def _dot_nt(a, b, precision=None):
    return lax.dot_general(a, b, (((1,), (1,)), ((), ())), preferred_element_type=F32, precision=precision)


def _rms(x, g):
    return x * lax.rsqrt(jnp.mean(x * x, axis=-1, keepdims=True) + NORM_EPS) * g


_HI_MASK = -65536


def _pack_rows(xb):
    half = xb.shape[1] // 2
    lo = lax.bitcast_convert_type(xb[:, :half].astype(F32), I32)
    hi = lax.bitcast_convert_type(xb[:, half:].astype(F32), I32)
    return (hi & _HI_MASK) | lax.shift_right_logical(lo, jnp.full(lo.shape, 16, I32))


def _unpack_rows(w):
    lo = lax.bitcast_convert_type(lax.shift_left(w, jnp.full(w.shape, 16, I32)), F32)
    hi = lax.bitcast_convert_type(w & _HI_MASK, F32)
    return jnp.concatenate([lo, hi], axis=1).astype(BF16)


def _filter_kernel(z_ref, w1_ref, b1_ref, w2_ref, b2_ref, w3_ref, b3_ref, w4_ref, fr_ref, dl_ref, sd_ref):
    hp = lax.Precision.HIGHEST
    fr = fr_ref[...]
    z = z_ref[...]
    h = jnp.sin(fr * (jnp.dot(z, w1_ref[...], precision=hp, preferred_element_type=F32) + b1_ref[...]))
    h = jnp.sin(fr * (jnp.dot(h, w2_ref[...], precision=hp, preferred_element_type=F32) + b2_ref[...]))
    h = jnp.sin(fr * (jnp.dot(h, w3_ref[...], precision=hp, preferred_element_type=F32) + b3_ref[...]))
    h4 = jnp.dot(h, w4_ref[...], precision=hp, preferred_element_type=F32)
    L = z.shape[0]
    db = dl_ref.shape[1]
    t = z[:, 0:1]
    decay = jnp.exp(-t * dl_ref[...])
    hf = h4[:, :db] * decay
    hb = h4[:, db:] * decay
    norm = jnp.sum(jnp.abs(hf), axis=0, keepdims=True) + jnp.sum(jnp.abs(hb), axis=0, keepdims=True)
    hf = hf / norm
    hb = hb / norm
    row = lax.broadcasted_iota(I32, (L, 1), 0)
    hbs = jnp.where(row == 0, 0.0, pltpu.roll(hb, 1, axis=0))
    s = hf + hbs
    d = hf - hbs
    s_hi = s.astype(BF16)
    d_hi = d.astype(BF16)
    sd_ref[:, 0 * db:1 * db] = s_hi
    sd_ref[:, 1 * db:2 * db] = (s - s_hi.astype(F32)).astype(BF16)
    sd_ref[:, 2 * db:3 * db] = d_hi
    sd_ref[:, 3 * db:4 * db] = (d - d_hi.astype(F32)).astype(BF16)


def _spec_kernel(wf_ref, sd_ref, ka_ref, kb_ref):
    db = ka_ref.shape[1]
    ks = _dot(wf_ref[...], sd_ref[...])
    ka_ref[...] = ks[:, 0:db] + ks[:, db:2 * db]
    kb_ref[...] = ks[:, 2 * db:3 * db] + ks[:, 3 * db:4 * db]


def _dft_tables(L):
    n2 = 2 * L
    k = jnp.arange(L, dtype=I32)[:, None]
    n = jnp.arange(L, dtype=I32)[None, :]
    ph = ((k * n) % n2).astype(F32) * (2.0 * math.pi / n2)
    c = jnp.cos(ph)
    s = jnp.sin(ph)
    alt = jnp.where(n % 2 == 0, 1.0, -1.0).astype(F32)
    wf_im = jnp.where(k == 0, alt, -s)
    wf = jnp.concatenate([c, wf_im], axis=0).astype(BF16)
    ck = jnp.where(k == 0, 1.0, 2.0) / n2
    wi_re = (c * ck).T
    wi_im = jnp.where(k == 0, alt / n2, -s * ck).T
    wi = jnp.concatenate([wi_re, wi_im], axis=1).astype(BF16)
    return wf, wi


def _filter_spectrum(L, w1, b1, w2, b2, w3, b3, w4, freq, wf):
    pe, ffn = w1.shape
    db = w4.shape[1] // 2
    bands = (pe - 1) // 2
    t = jnp.linspace(0.0, 1.0, L, dtype=F32)[:, None]
    w = 2.0 * math.pi * jnp.arange(L, dtype=F32)[:, None] / L
    bnd = jnp.linspace(1e-4, bands - 1, bands, dtype=F32)[None, :]
    z = jnp.concatenate([t, jnp.cos(bnd * w), -jnp.sin(bnd * w)], axis=-1)
    P = 128
    zp = jnp.zeros((L, P), F32).at[:, :pe].set(z)
    pad2 = lambda a, r, c: jnp.zeros((r, c), F32).at[:a.shape[0], :a.shape[1]].set(a.astype(F32))
    max_decay = math.log(DECAY_TARGET) / FAST_DECAY_PCT
    min_decay = math.log(DECAY_TARGET) / SLOW_DECAY_PCT
    deltas = jnp.abs(jnp.linspace(min_decay, max_decay, db, dtype=F32))[None, :]
    sd = pl.pallas_call(
        _filter_kernel,
        out_shape=jax.ShapeDtypeStruct((L, 4 * db), BF16),
        compiler_params=pltpu.CompilerParams(vmem_limit_bytes=VMEM_LIMIT),
    )(zp, pad2(w1, P, P), pad2(b1[None], 1, P), pad2(w2, P, P), pad2(b2[None], 1, P), pad2(w3, P, P),
      pad2(b3[None], 1, P), pad2(w4, P, 2 * db), pad2(freq[None], 1, P), deltas)
    rb = min(512, L)
    ka, kb = pl.pallas_call(
        _spec_kernel,
        grid=(2 * L // rb,),
        in_specs=[pl.BlockSpec((rb, L), lambda i: (i, 0)), pl.BlockSpec((L, 4 * db), lambda i: (0, 0))],
        out_specs=[pl.BlockSpec((rb, db), lambda i: (i, 0)), pl.BlockSpec((rb, db), lambda i: (i, 0))],
        out_shape=[jax.ShapeDtypeStruct((2 * L, db), F32)] * 2,
        compiler_params=_cparams(("parallel",)),
    )(wf, sd)
    kre = ka[:L]
    kim = jnp.concatenate([ka[L:L + 1], kb[L + 1:]], axis=0)
    return kre, kim


def _inproj_kernel(x_ref, g_ref, w_ref, b_ref, a_ref, hy_ref, gate_ref, *, da, db):
    h = _rms(x_ref[...], g_ref[...]).astype(BF16)
    c0, c1, c2 = da, 2 * da, 2 * da + 3 * db
    za = _dot(h, w_ref[:, 0:c0]) + b_ref[:, 0:c0]
    zg = _dot(h, w_ref[:, c0:c1]) + b_ref[:, c0:c1]
    a_ref[...] = (za * jax.nn.sigmoid(zg)).astype(BF16)
    hy_ref[...] = (_dot(h, w_ref[:, c1:c2]) + b_ref[:, c1:c2]).astype(BF16)
    gate_ref[...] = jax.nn.sigmoid(_dot(h, w_ref[:, c2:]) + b_ref[:, c2:]).astype(BF16)


def _inproj(x, g, w, b, da, db):
    T, D = x.shape
    C = w.shape[1]
    tm = min(TOKEN_TILE, T)
    ng = C - 2 * da - 3 * db
    row = lambda i: (i, 0)
    fix = lambda i: (0, 0)
    return pl.pallas_call(
        functools.partial(_inproj_kernel, da=da, db=db),
        grid=(T // tm,),
        in_specs=[pl.BlockSpec((tm, D), row), pl.BlockSpec((1, D), fix), pl.BlockSpec((D, C), fix),
                  pl.BlockSpec((1, C), fix)],
        out_specs=[pl.BlockSpec((tm, da), row), pl.BlockSpec((tm, 3 * db), row), pl.BlockSpec((tm, ng), row)],
        out_shape=[jax.ShapeDtypeStruct((T, da), BF16), jax.ShapeDtypeStruct((T, 3 * db), BF16),
                   jax.ShapeDtypeStruct((T, ng), BF16)],
        compiler_params=_cparams(("parallel",)),
    )(x, g, w, b)


def _conv_kernel(a_ref, hy_ref, cw_ref, cb_ref, lg_ref, lb_ref, sw_ref, sb_ref,
                 act_ref, x0_ref, vv_ref, pa_ref, ph_ref, sh_ref, cv_ref, *, L, da, db):
    K = cw_ref.shape[0]
    KS = sw_ref.shape[0]
    R = CONV_ROWS
    P = CONV_PAD
    LANES = 128
    zeros_a = jnp.zeros((P, da), F32)
    zeros_h = jnp.zeros((P, 3 * db), F32)
    pa_ref[0:P, :] = zeros_a
    pa_ref[P + L:P + L + P, :] = zeros_a
    ph_ref[0:P, :] = zeros_h
    ph_ref[P + L:P + L + P, :] = zeros_h
    pa_ref[P:P + L, :] = a_ref[0].astype(F32)
    ph_ref[P:P + L, :] = hy_ref[0].astype(F32)

    def chunk(c, carry):
        base = pl.multiple_of(c * R, R)
        for cb in range(da // LANES):
            cols = slice(cb * LANES, (cb + 1) * LANES)
            win = pa_ref[pl.ds(base, R + 2 * P), cols]
            for r in range(8):
                sh_ref[r] = win[r:r + R + 2 * P - 8, :]
            acc = jnp.zeros((R, LANES), F32) + cb_ref[:, cols]
            for k in range(K):
                q, r = divmod(P - K // 2 + k, 8)
                acc = acc + cw_ref[k:k + 1, cols] * sh_ref[r, 8 * q:8 * q + R, :]
            cv_ref[:, cols] = acc
        acc = cv_ref[...]
        mu = jnp.mean(acc, axis=-1, keepdims=True)
        xc = acc - mu
        var = jnp.mean(xc * xc, axis=-1, keepdims=True)
        y = xc * lax.rsqrt(var + LN_EPS) * lg_ref[...] + lb_ref[...]
        act_ref[0, pl.ds(base, R), :] = (y * jax.nn.sigmoid(y)).astype(BF16)

        def short(lo):
            cols = slice(lo, lo + LANES)
            win = ph_ref[pl.ds(base, R + 2 * P), cols]
            u = jnp.zeros((R, LANES), F32) + sb_ref[:, cols]
            for k in range(KS):
                o = P - KS // 2 + k
                u = u + sw_ref[k:k + 1, cols] * win[o:o + R, :]
            return u

        for cb in range(db // LANES):
            lo = cb * LANES
            cols = slice(lo, lo + LANES)
            vv_ref[0, pl.ds(base, R), cols] = (short(db + lo) * short(2 * db + lo)).astype(BF16)
            x0_ref[0, pl.ds(base, R), cols] = short(lo).astype(BF16)
        return carry

    lax.fori_loop(0, L // R, chunk, 0)


def _convs(a, hy, cw, cb, lg, lb, sw, sb):
    B, L, da = a.shape
    db = hy.shape[2] // 3
    bat = lambda b: (b, 0, 0)
    fix = lambda b: (0, 0)
    return pl.pallas_call(
        functools.partial(_conv_kernel, L=L, da=da, db=db),
        grid=(B,),
        in_specs=[pl.BlockSpec((1, L, da), bat), pl.BlockSpec((1, L, 3 * db), bat),
                  pl.BlockSpec(cw.shape, fix), pl.BlockSpec((1, da), fix), pl.BlockSpec((1, da), fix),
                  pl.BlockSpec((1, da), fix), pl.BlockSpec(sw.shape, fix), pl.BlockSpec((1, 3 * db), fix)],
        out_specs=[pl.BlockSpec((1, L, da), bat), pl.BlockSpec((1, L, db), bat), pl.BlockSpec((1, L, db), bat)],
        out_shape=[jax.ShapeDtypeStruct((B, L, da), BF16), jax.ShapeDtypeStruct((B, L, db), BF16),
                   jax.ShapeDtypeStruct((B, L, db), BF16)],
        scratch_shapes=[pltpu.VMEM((L + 2 * CONV_PAD, da), F32), pltpu.VMEM((L + 2 * CONV_PAD, 3 * db), F32),
                        pltpu.VMEM((8, CONV_ROWS + 2 * CONV_PAD - 8, 128), F32), pltpu.VMEM((CONV_ROWS, da), F32)],
        compiler_params=_cparams(("parallel",)),
    )(a, hy, cw, cb, lg, lb, sw, sb)


def _lconv_kernel(v_ref, x0_ref, wfr_ref, wfi_ref, kre_ref, kim_ref, wir_ref, wii_ref, skip_ref,
                  u_ref, acc_ref, *, nfb, fb):
    j = pl.program_id(1)
    v = v_ref[0]
    xre = _dot(wfr_ref[...], v)
    xim = _dot(wfi_ref[...], v)
    kre = kre_ref[...]
    kim = kim_ref[...]
    row0 = (lax.broadcasted_iota(I32, (fb, 1), 0) + j * fb) == 0
    t = xim * kim
    pre = xre * kre - jnp.where(row0, 0.0, t)
    pim = jnp.where(row0, t, xre * kim + xim * kre)
    contrib = _dot(wir_ref[...], pre.astype(BF16)) + _dot(wii_ref[...], pim.astype(BF16))

    @pl.when(j == 0)
    def _():
        acc_ref[...] = contrib

    @pl.when(j > 0)
    def _():
        acc_ref[...] += contrib

    @pl.when(j == nfb - 1)
    def _():
        v2 = acc_ref[...] + skip_ref[...] * v.astype(F32)
        u_ref[0] = (x0_ref[0].astype(F32) * v2).astype(BF16)


def _long_conv(vv, x0, wf, wi, kre, kim, skip):
    B, L, db = vv.shape
    fb = min(512, L)
    nfb = L // fb
    bat = lambda b, j: (b, 0, 0)
    return pl.pallas_call(
        functools.partial(_lconv_kernel, nfb=nfb, fb=fb),
        grid=(B, nfb),
        in_specs=[pl.BlockSpec((1, L, db), bat), pl.BlockSpec((1, L, db), bat),
                  pl.BlockSpec((fb, L), lambda b, j: (j, 0)), pl.BlockSpec((fb, L), lambda b, j: (nfb + j, 0)),
                  pl.BlockSpec((fb, db), lambda b, j: (j, 0)), pl.BlockSpec((fb, db), lambda b, j: (j, 0)),
                  pl.BlockSpec((L, fb), lambda b, j: (0, j)), pl.BlockSpec((L, fb), lambda b, j: (0, nfb + j)),
                  pl.BlockSpec((1, db), lambda b, j: (0, 0))],
        out_specs=pl.BlockSpec((1, L, db), bat),
        out_shape=jax.ShapeDtypeStruct((B, L, db), BF16),
        scratch_shapes=[pltpu.VMEM((L, db), F32)],
        compiler_params=_cparams(("parallel", "arbitrary")),
    )(vv, x0, wf, wf, kre, kim, wi, wi, skip)


def _post_kernel(x_ref, act_ref, u_ref, gate_ref, wa_ref, ba_ref, wb_ref, bb_ref, wo_ref, n2_ref, rwt_ref,
                 x1_ref, tok_ref, aff_ref):
    D = x_ref.shape[1]
    ya = _dot(act_ref[...], wa_ref[...]) + ba_ref[...]
    yb = _dot(u_ref[...], wb_ref[...]) + bb_ref[...]
    g = gate_ref[...].astype(F32)
    m = g[:, :D] * ya + g[:, D:] * yb
    x1 = x_ref[...] + _dot(m.astype(BF16), wo_ref[...])
    x1_ref[...] = x1
    h2 = _rms(x1, n2_ref[...]).astype(BF16)
    tok_ref[...] = _pack_rows(h2)
    logits = _dot_nt(rwt_ref[...], h2)
    ex = jnp.exp(logits - jnp.max(logits, axis=0, keepdims=True))
    aff_ref[...] = ex / jnp.sum(ex, axis=0, keepdims=True)


def _post(x, act, u, gate, wa, ba, wb, bb, wo, n2, rwt):
    T, D = x.shape
    E = rwt.shape[0]
    tm = min(TOKEN_TILE, T)
    row = lambda i: (i, 0)
    fix = lambda i: (0, 0)
    full = lambda a: pl.BlockSpec(a.shape, fix)
    return pl.pallas_call(
        _post_kernel,
        grid=(T // tm,),
        in_specs=[pl.BlockSpec((tm, D), row), pl.BlockSpec((tm, act.shape[1]), row),
                  pl.BlockSpec((tm, u.shape[1]), row), pl.BlockSpec((tm, gate.shape[1]), row),
                  full(wa), full(ba), full(wb), full(bb), full(wo), full(n2), full(rwt)],
        out_specs=[pl.BlockSpec((tm, D), row), pl.BlockSpec((tm, D // 2), row),
                   pl.BlockSpec((E, tm), lambda i: (0, i))],
        out_shape=[jax.ShapeDtypeStruct((T, D), F32), jax.ShapeDtypeStruct((T, D // 2), I32),
                   jax.ShapeDtypeStruct((E, T), F32)],
        compiler_params=_cparams(("parallel",)),
    )(x, act, u, gate, wa, ba, wb, bb, wo, n2, rwt)


def _select_kernel(a_ref, idx_ref, gsel_ref, qloc_ref, ntab_ref, otab_ref, thr_ref, *, cap):
    E, nt, W = a_ref.shape
    nsb = cap // W
    bits = lax.bitcast_convert_type(a_ref[...], I32)

    def bisect(b, cur):
        cand = cur | jnp.left_shift(jnp.int32(1), 30 - b)
        hit = jnp.where(bits >= cand, 1.0, 0.0)
        cnt = jnp.sum(jnp.sum(hit, axis=2, keepdims=True), axis=1, keepdims=True)
        return jnp.where(cnt >= cap, cand, cur)

    thr_ref[...] = lax.fori_loop(0, 31, bisect, jnp.zeros((E, 1, 1), I32))

    r_i = lax.broadcasted_iota(I32, (W, W), 0)
    c_i = lax.broadcasted_iota(I32, (W, W), 1)
    upper = jnp.where(r_i <= c_i, 1.0, 0.0).astype(BF16)
    ones = jnp.ones((W, W), BF16)
    tr = lax.broadcasted_iota(I32, (nt, nt), 0)
    tc = lax.broadcasted_iota(I32, (nt, nt), 1)
    lower = jnp.where(tc < tr, 1.0, 0.0).astype(BF16)
    tile_id = lax.broadcasted_iota(I32, (nt, W), 0).astype(F32)
    lane = lax.broadcasted_iota(I32, (1, W), 1).astype(F32)

    def prefix(maskf):
        mb = maskf.astype(BF16)
        incl = _dot(mb, upper)
        tot = _dot(mb, ones)
        off = _dot(lower, tot.astype(BF16))
        return incl, tot, off

    def expert(e, carry):
        ae = a_ref[e]
        be = lax.bitcast_convert_type(ae, I32)
        th = thr_ref[e]
        gt = be > th
        eq = be == th
        gtf = jnp.where(gt, 1.0, 0.0)
        eqf = jnp.where(eq, 1.0, 0.0)
        need = cap - jnp.sum(jnp.sum(gtf, axis=1, keepdims=True), axis=0, keepdims=True)
        e_incl, _, e_off = prefix(eqf)
        eq_rank = e_off + e_incl - eqf
        sel = gt | (eq & (eq_rank < need))
        self_ = jnp.where(sel, 1.0, 0.0)
        incl, tot, off = prefix(self_)
        gsel_ref[e] = jnp.where(sel, ae, 0.0)
        qloc_ref[e] = incl - 1.0
        ntab_ref[e] = tot[:, 0:128]
        otab_ref[e] = off[:, 0:128]
        incl_t = incl.T.astype(BF16)
        off_hi = off + tot

        def slots(c, carry2):
            s_row = lane + jnp.asarray(c * W, F32)
            oh = jnp.where((off <= s_row) & (s_row < off_hi), 1.0, 0.0)
            tile = jnp.sum(oh * tile_id, axis=0, keepdims=True)
            rank = s_row - jnp.sum(oh * off, axis=0, keepdims=True)
            rt = _dot(incl_t, oh.astype(BF16))
            pos = jnp.sum(jnp.where(rt <= rank, 1.0, 0.0), axis=0, keepdims=True)
            idx_ref[e, pl.ds(c, 1), :] = (tile * W + pos).astype(I32)
            return carry2

        lax.fori_loop(0, nsb, slots, 0)
        return carry

    lax.fori_loop(0, E, expert, 0)


def _select(aff3, cap):
    E, nt, W = aff3.shape
    return pl.pallas_call(
        functools.partial(_select_kernel, cap=cap),
        out_shape=[jax.ShapeDtypeStruct((E, cap // W, W), I32), jax.ShapeDtypeStruct((E, nt, W), F32),
                   jax.ShapeDtypeStruct((E, nt, W), F32), jax.ShapeDtypeStruct((E, nt, 128), F32),
                   jax.ShapeDtypeStruct((E, nt, 128), F32)],
        scratch_shapes=[pltpu.VMEM((E, 1, 1), I32)],
        compiler_params=pltpu.CompilerParams(vmem_limit_bytes=VMEM_LIMIT),
    )(aff3)


SC_GATHER_ROWS = 64


def _sc_gather(table, idx):
    M = idx.shape[0]
    Dw = table.shape[1]
    win = SC_GATHER_ROWS
    info = plsc.get_sparse_core_info()
    nc = info.num_cores
    nw = nc * info.num_subcores
    per_w = M // nw
    assert M % (nw * win) == 0
    mesh = plsc.VectorSubcoreMesh(core_axis_name="c", subcore_axis_name="s")

    @functools.partial(
        pl.kernel, mesh=mesh,
        out_type=jax.ShapeDtypeStruct((M, Dw), table.dtype),
        scratch_types=[pltpu.VMEM((win,), I32), pltpu.VMEM((win, Dw), table.dtype), pltpu.SemaphoreType.DMA],
    )
    def gather(table_hbm, idx_hbm, out_hbm, idx_v, rows_v, sem):
        wid = lax.axis_index("s") * nc + lax.axis_index("c")

        @pl.loop(0, per_w // win)
        def _(j):
            base = pl.multiple_of(wid * per_w + j * win, win)
            pltpu.sync_copy(idx_hbm.at[pl.ds(base, win)], idx_v)
            pltpu.async_copy(table_hbm.at[idx_v], rows_v, sem).wait()
            pltpu.sync_copy(rows_v, out_hbm.at[pl.ds(base, win)])

    return gather(table, idx)


def _ffn_kernel(xs_ref, wg_ref, wu_ref, wd_ref, ye_ref):
    xs = _unpack_rows(xs_ref[...])
    g = _dot(xs, wg_ref[0])
    u = _dot(xs, wu_ref[0])
    hid = (g * jax.nn.sigmoid(g) * u).astype(BF16)
    ye_ref[...] = _dot(hid, wd_ref[0])


def _expert_ffn(xs, wg, wu, wd):
    E, D, F = wg.shape
    R = ROUTE_TILE
    nsb = xs.shape[0] // (E * R)
    return pl.pallas_call(
        _ffn_kernel,
        grid=(E, nsb),
        in_specs=[pl.BlockSpec((R, D // 2), lambda e, s: (e * nsb + s, 0)),
                  pl.BlockSpec((1, D, F), lambda e, s: (e, 0, 0)), pl.BlockSpec((1, D, F), lambda e, s: (e, 0, 0)),
                  pl.BlockSpec((1, F, D), lambda e, s: (e, 0, 0))],
        out_specs=pl.BlockSpec((R, D), lambda e, s: (e * nsb + s, 0)),
        out_shape=jax.ShapeDtypeStruct((xs.shape[0], D), F32),
        compiler_params=_cparams(("parallel", "parallel")),
    )(xs, wg, wu, wd)


_SEG_SIZES = tuple(ROUTE_TILE >> b for b in range(ROUTE_TILE.bit_length() - 3))
_COMBINE_ROWS_PER_EXPERT = ROUTE_TILE + 16


def _combine_kernel(ntab, otab, x1_ref, gsel_ref, qloc_ref, ye_hbm, g_ref, y_ref, buf, sem, *, nt, cap, E):
    W = ROUTE_TILE
    CH = COMBINE_CHUNK
    D = x1_ref.shape[1]
    i = pl.program_id(0)

    def segment_copies(action):
        cum = jnp.int32(0)
        starts, lens = [], []
        for e in range(E):
            n = ntab[e * nt + i]
            src = otab[e * nt + i] + e * cap
            lead = src % 8
            src8 = src - lead
            len8 = ((n + lead + 7) // 8) * 8
            len8 = jnp.where(n > 0, len8, 0)
            starts.append(cum + lead)
            lens.append(n)
            done = jnp.int32(0)
            for size in _SEG_SIZES:
                take = (len8 & size) != 0

                @pl.when(take)
                def _(src8=src8, dst=cum, done=done, size=size):
                    cp = pltpu.make_async_copy(ye_hbm.at[pl.ds(pl.multiple_of(src8 + done, 8), size)],
                                               buf.at[pl.ds(pl.multiple_of(dst + done, 8), size)], sem.at[0])
                    action(cp)

                done = done + jnp.where(take, size, 0)
            cum = cum + len8
        return starts, lens, cum

    segment_copies(lambda cp: cp.start())
    starts, lens, m = segment_copies(lambda cp: cp.wait())

    erow = lax.broadcasted_iota(I32, (E, 1), 0)
    cume = jnp.zeros((E, 1), F32)
    for e in range(E):
        cume = jnp.where(erow == e, starts[e].astype(F32), cume)
    qg = qloc_ref[0] + cume
    eye = jnp.where(lax.broadcasted_iota(I32, (W, W), 0) == lax.broadcasted_iota(I32, (W, W), 1), 1.0, 0.0)
    hp = lax.Precision.HIGHEST
    qg_t = _dot_nt(eye, qg, precision=hp)
    gs_t = _dot_nt(eye, gsel_ref[0], precision=hp)

    def chunk(c, acc):
        base = pl.multiple_of(c * CH, CH)
        rid = base + lax.broadcasted_iota(I32, (CH, 1), 0)
        used = rid < 0
        for e in range(E):
            used = used | ((rid >= starts[e]) & (rid < starts[e] + lens[e]))
        rows = jnp.where(used, buf[pl.ds(base, CH), :], 0.0).astype(BF16)
        qrow = (base + lax.broadcasted_iota(I32, (1, CH), 1)).astype(F32)
        s = jnp.zeros((W, CH), F32)
        for e in range(E):
            s = s + jnp.where(qg_t[:, e:e + 1] == qrow, gs_t[:, e:e + 1], 0.0)
        return acc + _dot(s.astype(BF16), rows)

    moe = lax.fori_loop(0, (m + CH - 1) // CH, chunk, jnp.zeros((W, D), F32))
    y_ref[...] = _rms(x1_ref[...] + moe, g_ref[...])


def _combine(ntab, otab, x1, gsel_t, qloc_t, ye, g, cap):
    T, D = x1.shape
    nt, E, W = gsel_t.shape
    grid_spec = pltpu.PrefetchScalarGridSpec(
        num_scalar_prefetch=2,
        grid=(nt,),
        in_specs=[pl.BlockSpec((W, D), lambda i, *_: (i, 0)),
                  pl.BlockSpec((1, E, W), lambda i, *_: (i, 0, 0)),
                  pl.BlockSpec((1, E, W), lambda i, *_: (i, 0, 0)),
                  pl.BlockSpec(memory_space=pl.ANY),
                  pl.BlockSpec((1, D), lambda i, *_: (0, 0))],
        out_specs=pl.BlockSpec((W, D), lambda i, *_: (i, 0)),
        scratch_shapes=[pltpu.VMEM((pl.cdiv(E * _COMBINE_ROWS_PER_EXPERT, COMBINE_CHUNK) * COMBINE_CHUNK, D), F32),
                        pltpu.SemaphoreType.DMA((1,))],
    )
    return pl.pallas_call(
        functools.partial(_combine_kernel, nt=nt, cap=cap, E=E),
        grid_spec=grid_spec,
        out_shape=jax.ShapeDtypeStruct((T, D), F32),
        compiler_params=_cparams(("arbitrary",)),
    )(ntab, otab, x1, gsel_t, qloc_t, ye, g)


def _layer(x, p, spec_cache):
    B, L, D = x.shape
    T = B * L
    da = p["conv_dw_w"].shape[1]
    db = p["hy_skip"].shape[1]
    E = p["router_wt"].shape[0]
    W = ROUTE_TILE
    cap = max(1, min(T, CAPACITY_FACTOR * T // E))
    assert T % TOKEN_TILE == 0 and cap % W == 0 and L % CONV_ROWS == 0

    if L not in spec_cache:
        wf, wi = _dft_tables(L)
        kre, kim = _filter_spectrum(L, p["hy_f_w1"], p["hy_f_b1"], p["hy_f_w2"], p["hy_f_b2"], p["hy_f_w3"],
                                    p["hy_f_b3"], p["hy_f_w4"], p["hy_f_freq"], wf)
        spec_cache[L] = (wf, wi, kre, kim)
    wf, wi, kre, kim = spec_cache[L]

    x2 = x.reshape(T, D)
    a, hy, gate = _inproj(x2, p["norm1_g"], p["w_in"], p["b_in"], da, db)
    act, x0, vv = _convs(a.reshape(B, L, da), hy.reshape(B, L, 3 * db), p["conv_dw_w"], p["conv_dw_b"],
                         p["conv_ln_g"], p["conv_ln_b"], p["hy_short_w"], p["hy_short_b"])
    u = _long_conv(vv, x0, wf, wi, kre, kim, p["hy_skip"])
    x1, tok, aff = _post(x2, act.reshape(T, da), u.reshape(T, db), gate, p["conv_w_out"], p["conv_b_out"],
                         p["hy_w_out"], p["hy_b_out"], p["w_o"], p["norm2_g"], p["router_wt"])
    nt = T // W
    idx, gsel, qloc, ntab, otab = _select(aff.reshape(E, nt, W), cap)
    xs = _sc_gather(tok, idx.reshape(E * cap))
    ye = _expert_ffn(xs, p["exp_w_gate"], p["exp_w_up"], p["exp_w_down"])
    ntab_i = ntab[:, :, 0].astype(I32).reshape(E * nt)
    otab_i = otab[:, :, 0].astype(I32).reshape(E * nt)
    y = _combine(ntab_i, otab_i, x1, gsel.transpose(1, 0, 2), qloc.transpose(1, 0, 2), ye, p["norm_f_g"], cap)
    return y.reshape(B, L, D)


def kernel(x_prompt, x_sample, norm1_g, w_in, b_in, conv_dw_w, conv_dw_b, conv_ln_g, conv_ln_b, conv_w_out, conv_b_out, hy_short_w, hy_short_b, hy_f_w1, hy_f_b1, hy_f_w2, hy_f_b2, hy_f_w3, hy_f_b3, hy_f_w4, hy_f_freq, hy_skip, hy_w_out, hy_b_out, w_o, norm2_g, router_w, exp_w_gate, exp_w_up, exp_w_down, norm_f_g):
    depth = w_in.shape[0]
    assert depth == 1
    i = 0
    row = lambda a: a[i][None].astype(F32)
    p = {
        "norm1_g": row(norm1_g), "w_in": w_in[i].astype(BF16), "b_in": row(b_in),
        "conv_dw_w": conv_dw_w[i].astype(F32), "conv_dw_b": row(conv_dw_b),
        "conv_ln_g": row(conv_ln_g), "conv_ln_b": row(conv_ln_b),
        "conv_w_out": conv_w_out[i].astype(BF16), "conv_b_out": row(conv_b_out),
        "hy_short_w": hy_short_w[i].astype(F32), "hy_short_b": row(hy_short_b),
        "hy_f_w1": hy_f_w1[i], "hy_f_b1": hy_f_b1[i], "hy_f_w2": hy_f_w2[i], "hy_f_b2": hy_f_b2[i],
        "hy_f_w3": hy_f_w3[i], "hy_f_b3": hy_f_b3[i], "hy_f_w4": hy_f_w4[i], "hy_f_freq": hy_f_freq[i],
        "hy_skip": row(hy_skip), "hy_w_out": hy_w_out[i].astype(BF16), "hy_b_out": row(hy_b_out),
        "w_o": w_o[i].astype(BF16), "norm2_g": row(norm2_g), "router_wt": router_w[i].T.astype(BF16),
        "exp_w_gate": exp_w_gate[i].astype(BF16), "exp_w_up": exp_w_up[i].astype(BF16),
        "exp_w_down": exp_w_down[i].astype(BF16), "norm_f_g": norm_f_g[None].astype(F32),
    }
    spec_cache = {}
    return (_layer(x_prompt, p, spec_cache), _layer(x_sample, p, spec_cache))
```

```python
import functools
import math

import jax
import jax.numpy as jnp
from jax import lax
from jax.experimental import pallas as pl
from jax.experimental.pallas import tpu as pltpu
from jax.experimental.pallas import tpu_sc as plsc

F32 = jnp.float32
BF16 = jnp.bfloat16
I32 = jnp.int32

NORM_EPS = 1e-6
LN_EPS = 1e-5
FAST_DECAY_PCT = 0.3
SLOW_DECAY_PCT = 1.5
DECAY_TARGET = 1e-2
CAPACITY_FACTOR = 2

ROUTE_TILE = 256
TOKEN_TILE = 512
CONV_ROWS = 128
CONV_PAD = 16
COMBINE_TILE = 512
SIDE_LANES = 128
VMEM_LIMIT = 56 * 1024 * 1024


def _cparams(sem):
    return pltpu.CompilerParams(dimension_semantics=sem, vmem_limit_bytes=VMEM_LIMIT)


def _dot(a, b):
    return jnp.dot(a, b, preferred_element_type=F32)


def _dot_nt(a, b, precision=None):
    return lax.dot_general(a, b, (((1,), (1,)), ((), ())), preferred_element_type=F32, precision=precision)


def _rms(x, g):
    return x * lax.rsqrt(jnp.mean(x * x, axis=-1, keepdims=True) + NORM_EPS) * g


_HI_MASK = -65536


def _pack_rows(xb):
    half = xb.shape[1] // 2
    lo = lax.bitcast_convert_type(xb[:, :half].astype(F32), I32)
    hi = lax.bitcast_convert_type(xb[:, half:].astype(F32), I32)
    return (hi & _HI_MASK) | lax.shift_right_logical(lo, jnp.full(lo.shape, 16, I32))


def _unpack_rows(w):
    lo = lax.bitcast_convert_type(lax.shift_left(w, jnp.full(w.shape, 16, I32)), F32)
    hi = lax.bitcast_convert_type(w & _HI_MASK, F32)
    return jnp.concatenate([lo, hi], axis=1).astype(BF16)


def _filter_kernel(z_ref, w1_ref, b1_ref, w2_ref, b2_ref, w3_ref, b3_ref, w4_ref, fr_ref, dl_ref, sd_ref):
    hp = lax.Precision.HIGHEST
    fr = fr_ref[...]
    z = z_ref[...]
    h = jnp.sin(fr * (jnp.dot(z, w1_ref[...], precision=hp, preferred_element_type=F32) + b1_ref[...]))
    h = jnp.sin(fr * (jnp.dot(h, w2_ref[...], precision=hp, preferred_element_type=F32) + b2_ref[...]))
    h = jnp.sin(fr * (jnp.dot(h, w3_ref[...], precision=hp, preferred_element_type=F32) + b3_ref[...]))
    h4 = jnp.dot(h, w4_ref[...], precision=hp, preferred_element_type=F32)
    L = z.shape[0]
    db = dl_ref.shape[1]
    t = z[:, 0:1]
    decay = jnp.exp(-t * dl_ref[...])
    hf = h4[:, :db] * decay
    hb = h4[:, db:] * decay
    norm = jnp.sum(jnp.abs(hf), axis=0, keepdims=True) + jnp.sum(jnp.abs(hb), axis=0, keepdims=True)
    hf = hf / norm
    hb = hb / norm
    row = lax.broadcasted_iota(I32, (L, 1), 0)
    hbs = jnp.where(row == 0, 0.0, pltpu.roll(hb, 1, axis=0))
    s = hf + hbs
    d = hf - hbs
    s_hi = s.astype(BF16)
    d_hi = d.astype(BF16)
    sd_ref[:, 0 * db:1 * db] = s_hi
    sd_ref[:, 1 * db:2 * db] = (s - s_hi.astype(F32)).astype(BF16)
    sd_ref[:, 2 * db:3 * db] = d_hi
    sd_ref[:, 3 * db:4 * db] = (d - d_hi.astype(F32)).astype(BF16)


def _spec_kernel(wf_ref, sd_ref, ka_ref, kb_ref):
    db = ka_ref.shape[1]
    ks = _dot(wf_ref[...], sd_ref[...])
    ka_ref[...] = ks[:, 0:db] + ks[:, db:2 * db]
    kb_ref[...] = ks[:, 2 * db:3 * db] + ks[:, 3 * db:4 * db]


def _dft_tables(L):
    n2 = 2 * L
    k = jnp.arange(L, dtype=I32)[:, None]
    n = jnp.arange(L, dtype=I32)[None, :]
    ph = ((k * n) % n2).astype(F32) * (2.0 * math.pi / n2)
    c = jnp.cos(ph)
    s = jnp.sin(ph)
    alt = jnp.where(n % 2 == 0, 1.0, -1.0).astype(F32)
    wf_im = jnp.where(k == 0, alt, -s)
    wf = jnp.concatenate([c, wf_im], axis=0).astype(BF16)
    ck = jnp.where(k == 0, 1.0, 2.0) / n2
    wi_re = (c * ck).T
    wi_im = jnp.where(k == 0, alt / n2, -s * ck).T
    wi = jnp.concatenate([wi_re, wi_im], axis=1).astype(BF16)
    return wf, wi


def _filter_spectrum(L, w1, b1, w2, b2, w3, b3, w4, freq, wf):
    pe, ffn = w1.shape
    db = w4.shape[1] // 2
    bands = (pe - 1) // 2
    t = jnp.linspace(0.0, 1.0, L, dtype=F32)[:, None]
    w = 2.0 * math.pi * jnp.arange(L, dtype=F32)[:, None] / L
    bnd = jnp.linspace(1e-4, bands - 1, bands, dtype=F32)[None, :]
    z = jnp.concatenate([t, jnp.cos(bnd * w), -jnp.sin(bnd * w)], axis=-1)
    P = 128
    zp = jnp.zeros((L, P), F32).at[:, :pe].set(z)
    pad2 = lambda a, r, c: jnp.zeros((r, c), F32).at[:a.shape[0], :a.shape[1]].set(a.astype(F32))
    max_decay = math.log(DECAY_TARGET) / FAST_DECAY_PCT
    min_decay = math.log(DECAY_TARGET) / SLOW_DECAY_PCT
    deltas = jnp.abs(jnp.linspace(min_decay, max_decay, db, dtype=F32))[None, :]
    sd = pl.pallas_call(
        _filter_kernel,
        out_shape=jax.ShapeDtypeStruct((L, 4 * db), BF16),
        compiler_params=pltpu.CompilerParams(vmem_limit_bytes=VMEM_LIMIT),
    )(zp, pad2(w1, P, P), pad2(b1[None], 1, P), pad2(w2, P, P), pad2(b2[None], 1, P), pad2(w3, P, P),
      pad2(b3[None], 1, P), pad2(w4, P, 2 * db), pad2(freq[None], 1, P), deltas)
    rb = min(512, L)
    ka, kb = pl.pallas_call(
        _spec_kernel,
        grid=(2 * L // rb,),
        in_specs=[pl.BlockSpec((rb, L), lambda i: (i, 0)), pl.BlockSpec((L, 4 * db), lambda i: (0, 0))],
        out_specs=[pl.BlockSpec((rb, db), lambda i: (i, 0)), pl.BlockSpec((rb, db), lambda i: (i, 0))],
        out_shape=[jax.ShapeDtypeStruct((2 * L, db), F32)] * 2,
        compiler_params=_cparams(("parallel",)),
    )(wf, sd)
    kre = ka[:L]
    kim = jnp.concatenate([ka[L:L + 1], kb[L + 1:]], axis=0)
    return kre, kim


def _inproj_kernel(x_ref, g_ref, w_ref, b_ref, a_ref, hy_ref, gate_ref, *, da, db):
    h = _rms(x_ref[...], g_ref[...]).astype(BF16)
    c0, c1, c2 = da, 2 * da, 2 * da + 3 * db
    za = _dot(h, w_ref[:, 0:c0]) + b_ref[:, 0:c0]
    zg = _dot(h, w_ref[:, c0:c1]) + b_ref[:, c0:c1]
    a_ref[...] = (za * jax.nn.sigmoid(zg)).astype(BF16)
    hy_ref[...] = (_dot(h, w_ref[:, c1:c2]) + b_ref[:, c1:c2]).astype(BF16)
    gate_ref[...] = jax.nn.sigmoid(_dot(h, w_ref[:, c2:]) + b_ref[:, c2:]).astype(BF16)


def _inproj(x, g, w, b, da, db):
    T, D = x.shape
    C = w.shape[1]
    tm = min(TOKEN_TILE, T)
    ng = C - 2 * da - 3 * db
    row = lambda i: (i, 0)
    fix = lambda i: (0, 0)
    return pl.pallas_call(
        functools.partial(_inproj_kernel, da=da, db=db),
        grid=(T // tm,),
        in_specs=[pl.BlockSpec((tm, D), row), pl.BlockSpec((1, D), fix), pl.BlockSpec((D, C), fix),
                  pl.BlockSpec((1, C), fix)],
        out_specs=[pl.BlockSpec((tm, da), row), pl.BlockSpec((tm, 3 * db), row), pl.BlockSpec((tm, ng), row)],
        out_shape=[jax.ShapeDtypeStruct((T, da), BF16), jax.ShapeDtypeStruct((T, 3 * db), BF16),
                   jax.ShapeDtypeStruct((T, ng), BF16)],
        compiler_params=_cparams(("parallel",)),
    )(x, g, w, b)


def _conv_kernel(a_ref, hy_ref, cw_ref, cb_ref, lg_ref, lb_ref, sw_ref, sb_ref,
                 act_ref, x0_ref, vv_ref, pa_ref, ph_ref, sh_ref, cv_ref, *, L, da, db):
    K = cw_ref.shape[0]
    KS = sw_ref.shape[0]
    R = CONV_ROWS
    P = CONV_PAD
    LANES = 128
    zeros_a = jnp.zeros((P, da), F32)
    zeros_h = jnp.zeros((P, 3 * db), F32)
    pa_ref[0:P, :] = zeros_a
    pa_ref[P + L:P + L + P, :] = zeros_a
    ph_ref[0:P, :] = zeros_h
    ph_ref[P + L:P + L + P, :] = zeros_h
    pa_ref[P:P + L, :] = a_ref[0].astype(F32)
    ph_ref[P:P + L, :] = hy_ref[0].astype(F32)

    def chunk(c, carry):
        base = pl.multiple_of(c * R, R)
        for cb in range(da // LANES):
            cols = slice(cb * LANES, (cb + 1) * LANES)
            win = pa_ref[pl.ds(base, R + 2 * P), cols]
            for r in range(8):
                sh_ref[r] = win[r:r + R + 2 * P - 8, :]
            acc = jnp.zeros((R, LANES), F32) + cb_ref[:, cols]
            for k in range(K):
                q, r = divmod(P - K // 2 + k, 8)
                acc = acc + cw_ref[k:k + 1, cols] * sh_ref[r, 8 * q:8 * q + R, :]
            cv_ref[:, cols] = acc
        acc = cv_ref[...]
        mu = jnp.mean(acc, axis=-1, keepdims=True)
        xc = acc - mu
        var = jnp.mean(xc * xc, axis=-1, keepdims=True)
        y = xc * lax.rsqrt(var + LN_EPS) * lg_ref[...] + lb_ref[...]
        act_ref[0, pl.ds(base, R), :] = (y * jax.nn.sigmoid(y)).astype(BF16)

        def short(lo):
            cols = slice(lo, lo + LANES)
            win = ph_ref[pl.ds(base, R + 2 * P), cols]
            u = jnp.zeros((R, LANES), F32) + sb_ref[:, cols]
            for k in range(KS):
                o = P - KS // 2 + k
                u = u + sw_ref[k:k + 1, cols] * win[o:o + R, :]
            return u

        for cb in range(db // LANES):
            lo = cb * LANES
            cols = slice(lo, lo + LANES)
            vv_ref[0, pl.ds(base, R), cols] = (short(db + lo) * short(2 * db + lo)).astype(BF16)
            x0_ref[0, pl.ds(base, R), cols] = short(lo).astype(BF16)
        return carry

    lax.fori_loop(0, L // R, chunk, 0)


def _convs(a, hy, cw, cb, lg, lb, sw, sb):
    B, L, da = a.shape
    db = hy.shape[2] // 3
    bat = lambda b: (b, 0, 0)
    fix = lambda b: (0, 0)
    return pl.pallas_call(
        functools.partial(_conv_kernel, L=L, da=da, db=db),
        grid=(B,),
        in_specs=[pl.BlockSpec((1, L, da), bat), pl.BlockSpec((1, L, 3 * db), bat),
                  pl.BlockSpec(cw.shape, fix), pl.BlockSpec((1, da), fix), pl.BlockSpec((1, da), fix),
                  pl.BlockSpec((1, da), fix), pl.BlockSpec(sw.shape, fix), pl.BlockSpec((1, 3 * db), fix)],
        out_specs=[pl.BlockSpec((1, L, da), bat), pl.BlockSpec((1, L, db), bat), pl.BlockSpec((1, L, db), bat)],
        out_shape=[jax.ShapeDtypeStruct((B, L, da), BF16), jax.ShapeDtypeStruct((B, L, db), BF16),
                   jax.ShapeDtypeStruct((B, L, db), BF16)],
        scratch_shapes=[pltpu.VMEM((L + 2 * CONV_PAD, da), F32), pltpu.VMEM((L + 2 * CONV_PAD, 3 * db), F32),
                        pltpu.VMEM((8, CONV_ROWS + 2 * CONV_PAD - 8, 128), F32), pltpu.VMEM((CONV_ROWS, da), F32)],
        compiler_params=_cparams(("parallel",)),
    )(a, hy, cw, cb, lg, lb, sw, sb)


def _lconv_kernel(v_ref, x0_ref, wfr_ref, wfi_ref, kre_ref, kim_ref, wir_ref, wii_ref, skip_ref,
                  u_ref, acc_ref, *, nfb, fb):
    j = pl.program_id(1)
    v = v_ref[0]
    xre = _dot(wfr_ref[...], v)
    xim = _dot(wfi_ref[...], v)
    kre = kre_ref[...]
    kim = kim_ref[...]
    row0 = (lax.broadcasted_iota(I32, (fb, 1), 0) + j * fb) == 0
    t = xim * kim
    pre = xre * kre - jnp.where(row0, 0.0, t)
    pim = jnp.where(row0, t, xre * kim + xim * kre)
    contrib = _dot(wir_ref[...], pre.astype(BF16)) + _dot(wii_ref[...], pim.astype(BF16))

    @pl.when(j == 0)
    def _():
        acc_ref[...] = contrib

    @pl.when(j > 0)
    def _():
        acc_ref[...] += contrib

    @pl.when(j == nfb - 1)
    def _():
        v2 = acc_ref[...] + skip_ref[...] * v.astype(F32)
        u_ref[0] = (x0_ref[0].astype(F32) * v2).astype(BF16)


def _long_conv(vv, x0, wf, wi, kre, kim, skip):
    B, L, db = vv.shape
    fb = min(512, L)
    nfb = L // fb
    bat = lambda b, j: (b, 0, 0)
    return pl.pallas_call(
        functools.partial(_lconv_kernel, nfb=nfb, fb=fb),
        grid=(B, nfb),
        in_specs=[pl.BlockSpec((1, L, db), bat), pl.BlockSpec((1, L, db), bat),
                  pl.BlockSpec((fb, L), lambda b, j: (j, 0)), pl.BlockSpec((fb, L), lambda b, j: (nfb + j, 0)),
                  pl.BlockSpec((fb, db), lambda b, j: (j, 0)), pl.BlockSpec((fb, db), lambda b, j: (j, 0)),
                  pl.BlockSpec((L, fb), lambda b, j: (0, j)), pl.BlockSpec((L, fb), lambda b, j: (0, nfb + j)),
                  pl.BlockSpec((1, db), lambda b, j: (0, 0))],
        out_specs=pl.BlockSpec((1, L, db), bat),
        out_shape=jax.ShapeDtypeStruct((B, L, db), BF16),
        scratch_shapes=[pltpu.VMEM((L, db), F32)],
        compiler_params=_cparams(("parallel", "arbitrary")),
    )(vv, x0, wf, wf, kre, kim, wi, wi, skip)


def _post_kernel(x_ref, act_ref, u_ref, gate_ref, wa_ref, ba_ref, wb_ref, bb_ref, wo_ref, n2_ref, rw_ref,
                 x1_ref, tok_ref, *, E):
    tm, D = x_ref.shape
    ya = _dot(act_ref[...], wa_ref[...]) + ba_ref[...]
    yb = _dot(u_ref[...], wb_ref[...]) + bb_ref[...]
    g = gate_ref[...].astype(F32)
    m = g[:, :D] * ya + g[:, D:] * yb
    x1 = x_ref[...] + _dot(m.astype(BF16), wo_ref[...])
    x1_ref[...] = x1
    h2 = _rms(x1, n2_ref[...]).astype(BF16)
    tok_ref[:, :D // 2] = _pack_rows(h2)
    lane = lax.broadcasted_iota(I32, (tm, SIDE_LANES), 1)
    is_expert = (lane >= 1) & (lane <= E)
    logits = jnp.where(is_expert, _dot(h2, rw_ref[...]), -1e30)
    ex = jnp.exp(logits - jnp.max(logits, axis=1, keepdims=True))
    aff = jnp.where(is_expert, ex / jnp.sum(ex, axis=1, keepdims=True), 0.0)
    tok_id = (pl.program_id(0) * tm) % COMBINE_TILE + lax.broadcasted_iota(I32, (tm, SIDE_LANES), 0)
    tok_ref[:, D // 2:] = jnp.where(lane == 0, tok_id, lax.bitcast_convert_type(aff, I32))


def _post(x, act, u, gate, wa, ba, wb, bb, wo, n2, rw, E):
    T, D = x.shape
    tm = min(TOKEN_TILE, T)
    assert COMBINE_TILE % tm == 0
    RW = D // 2 + SIDE_LANES
    row = lambda i: (i, 0)
    fix = lambda i: (0, 0)
    full = lambda a: pl.BlockSpec(a.shape, fix)
    return pl.pallas_call(
        functools.partial(_post_kernel, E=E),
        grid=(T // tm,),
        in_specs=[pl.BlockSpec((tm, D), row), pl.BlockSpec((tm, act.shape[1]), row),
                  pl.BlockSpec((tm, u.shape[1]), row), pl.BlockSpec((tm, gate.shape[1]), row),
                  full(wa), full(ba), full(wb), full(bb), full(wo), full(n2), full(rw)],
        out_specs=[pl.BlockSpec((tm, D), row), pl.BlockSpec((tm, RW), row)],
        out_shape=[jax.ShapeDtypeStruct((T, D), F32), jax.ShapeDtypeStruct((T, RW), I32)],
        compiler_params=_cparams(("parallel",)),
    )(x, act, u, gate, wa, ba, wb, bb, wo, n2, rw)


def _select_kernel(a_ref, idx_ref, ntab_ref, otab_ref, thr_ref, *, cap):
    E, nt, W = a_ref.shape
    nsb = cap // W
    bits = lax.bitcast_convert_type(a_ref[...], I32)

    def bisect(b, cur):
        cand = cur | jnp.left_shift(jnp.int32(1), 30 - b)
        hit = jnp.where(bits >= cand, 1.0, 0.0)
        cnt = jnp.sum(jnp.sum(hit, axis=2, keepdims=True), axis=1, keepdims=True)
        return jnp.where(cnt >= cap, cand, cur)

    thr_ref[...] = lax.fori_loop(0, 31, bisect, jnp.zeros((E, 1, 1), I32))

    r_i = lax.broadcasted_iota(I32, (W, W), 0)
    c_i = lax.broadcasted_iota(I32, (W, W), 1)
    upper = jnp.where(r_i <= c_i, 1.0, 0.0).astype(BF16)
    ones = jnp.ones((W, W), BF16)
    tr = lax.broadcasted_iota(I32, (nt, nt), 0)
    tc = lax.broadcasted_iota(I32, (nt, nt), 1)
    lower = jnp.where(tc < tr, 1.0, 0.0).astype(BF16)
    tile_id = lax.broadcasted_iota(I32, (nt, W), 0).astype(F32)
    lane = lax.broadcasted_iota(I32, (1, W), 1).astype(F32)

    def prefix(maskf):
        mb = maskf.astype(BF16)
        incl = _dot(mb, upper)
        tot = _dot(mb, ones)
        off = _dot(lower, tot.astype(BF16))
        return incl, tot, off

    def expert(e, carry):
        ae = a_ref[e]
        be = lax.bitcast_convert_type(ae, I32)
        th = thr_ref[e]
        gt = be > th
        eq = be == th
        gtf = jnp.where(gt, 1.0, 0.0)
        eqf = jnp.where(eq, 1.0, 0.0)
        need = cap - jnp.sum(jnp.sum(gtf, axis=1, keepdims=True), axis=0, keepdims=True)
        e_incl, _, e_off = prefix(eqf)
        eq_rank = e_off + e_incl - eqf
        sel = gt | (eq & (eq_rank < need))
        self_ = jnp.where(sel, 1.0, 0.0)
        incl, tot, off = prefix(self_)
        ntab_ref[e] = tot[:, 0:128]
        otab_ref[e] = off[:, 0:128]
        incl_t = incl.T.astype(BF16)
        off_hi = off + tot

        def slots(c, carry2):
            s_row = lane + jnp.asarray(c * W, F32)
            oh = jnp.where((off <= s_row) & (s_row < off_hi), 1.0, 0.0)
            tile = jnp.sum(oh * tile_id, axis=0, keepdims=True)
            rank = s_row - jnp.sum(oh * off, axis=0, keepdims=True)
            rt = _dot(incl_t, oh.astype(BF16))
            pos = jnp.sum(jnp.where(rt <= rank, 1.0, 0.0), axis=0, keepdims=True)
            idx_ref[e, pl.ds(c, 1), :] = (tile * W + pos).astype(I32)
            return carry2

        lax.fori_loop(0, nsb, slots, 0)
        return carry

    lax.fori_loop(0, E, expert, 0)


def _select(aff3, cap):
    E, nt, W = aff3.shape
    return pl.pallas_call(
        functools.partial(_select_kernel, cap=cap),
        out_shape=[jax.ShapeDtypeStruct((E, cap // W, W), I32), jax.ShapeDtypeStruct((E, nt, 128), F32),
                   jax.ShapeDtypeStruct((E, nt, 128), F32)],
        scratch_shapes=[pltpu.VMEM((E, 1, 1), I32)],
        compiler_params=pltpu.CompilerParams(vmem_limit_bytes=VMEM_LIMIT),
    )(aff3)


SC_GATHER_ROWS = 64


def _sc_gather(table, idx):
    M = idx.shape[0]
    Dw = table.shape[1]
    win = SC_GATHER_ROWS
    info = plsc.get_sparse_core_info()
    nc = info.num_cores
    nw = nc * info.num_subcores
    per_w = M // nw
    assert M % (nw * win) == 0
    mesh = plsc.VectorSubcoreMesh(core_axis_name="c", subcore_axis_name="s")

    @functools.partial(
        pl.kernel, mesh=mesh,
        out_type=jax.ShapeDtypeStruct((M, Dw), table.dtype),
        scratch_types=[pltpu.VMEM((win,), I32), pltpu.VMEM((win, Dw), table.dtype), pltpu.SemaphoreType.DMA],
    )
    def gather(table_hbm, idx_hbm, out_hbm, idx_v, rows_v, sem):
        wid = lax.axis_index("s") * nc + lax.axis_index("c")

        @pl.loop(0, per_w // win)
        def _(j):
            base = pl.multiple_of(wid * per_w + j * win, win)
            pltpu.sync_copy(idx_hbm.at[pl.ds(base, win)], idx_v)
            pltpu.async_copy(table_hbm.at[idx_v], rows_v, sem).wait()
            pltpu.sync_copy(rows_v, out_hbm.at[pl.ds(base, win)])

    return gather(table, idx)


def _ffn_kernel(xs_ref, wg_ref, wu_ref, wd_ref, ye_ref):
    R = xs_ref.shape[0]
    half = wg_ref.shape[1] // 2
    xs = _unpack_rows(xs_ref[:, :half])
    side = xs_ref[:, half:]
    lane = lax.broadcasted_iota(I32, (R, SIDE_LANES), 1)
    aff = lax.bitcast_convert_type(side, F32)
    gate = jnp.sum(jnp.where(lane == pl.program_id(0) + 1, aff, 0.0), axis=1, keepdims=True)
    g = _dot(xs, wg_ref[0])
    u = _dot(xs, wu_ref[0])
    hid = (g * jax.nn.sigmoid(g) * u).astype(BF16)
    ye = _dot(hid, wd_ref[0]) * gate
    ye_ref[:, :half] = _pack_rows(ye.astype(BF16))
    ye_ref[:, half:] = side


def _expert_ffn(xs, wg, wu, wd):
    E, D, F = wg.shape
    R = ROUTE_TILE
    RW = xs.shape[1]
    nsb = xs.shape[0] // (E * R)
    return pl.pallas_call(
        _ffn_kernel,
        grid=(E, nsb),
        in_specs=[pl.BlockSpec((R, RW), lambda e, s: (e * nsb + s, 0)),
                  pl.BlockSpec((1, D, F), lambda e, s: (e, 0, 0)), pl.BlockSpec((1, D, F), lambda e, s: (e, 0, 0)),
                  pl.BlockSpec((1, F, D), lambda e, s: (e, 0, 0))],
        out_specs=pl.BlockSpec((R, RW), lambda e, s: (e * nsb + s, 0)),
        out_shape=jax.ShapeDtypeStruct(xs.shape, I32),
        compiler_params=_cparams(("parallel", "parallel")),
    )(xs, wg, wu, wd)


def _combine_tables(ntab, otab, cap):
    E, nt = ntab.shape
    W = ROUTE_TILE
    g = COMBINE_TILE // W
    ntc = nt // g
    n2 = ntab.reshape(E, ntc, g).sum(-1)
    off2 = otab.reshape(E, ntc, g)[:, :, 0]
    hi = jnp.cumsum(n2, axis=0)
    lo = hi - n2
    m = hi[-1]
    nch = jnp.maximum(1, (m + W - 1) // W)
    cb_hi = jnp.cumsum(nch)
    cb_lo = cb_hi - nch
    nc_max = (E * cap) // W + ntc
    c = jnp.arange(nc_max, dtype=I32)
    tile = jnp.minimum(jnp.searchsorted(cb_hi, c, side="right").astype(I32), ntc - 1)
    k = jnp.where(c < cb_hi[-1], c - cb_lo[tile], nch[ntc - 1] + c - cb_hi[-1])
    delta = jnp.arange(E, dtype=I32)[:, None] * cap + off2 - lo
    bcast = lambda a: jnp.broadcast_to(a.T.astype(F32)[:, :, None], (ntc, E, 128))
    return tile, k.astype(I32), m.astype(I32), nch.astype(I32), bcast(lo), bcast(hi), bcast(delta)


def _plan_kernel(tile_s, k_s, lo_ref, hi_ref, dl_ref, src_ref):
    W = src_ref.shape[1]
    lane = lax.broadcasted_iota(I32, (1, W), 1).astype(F32)

    def chunk(c, carry):
        t = tile_s[c]
        q = lane + jnp.asarray(k_s[c] * W, F32)
        lo = lo_ref[t][:, 0:1]
        hi = hi_ref[t][:, 0:1]
        mine = (lo <= q) & (q < hi)
        src = jnp.sum(jnp.where(mine, dl_ref[t][:, 0:1] + q, 0.0), axis=0, keepdims=True)
        src_ref[pl.ds(c, 1), :] = src.astype(I32)
        return carry

    lax.fori_loop(0, src_ref.shape[0], chunk, 0)


def _combine_plan(tile, k, lo, hi, delta):
    smem = pl.BlockSpec(memory_space=pltpu.SMEM)
    vmem = pl.BlockSpec(memory_space=pltpu.VMEM)
    return pl.pallas_call(
        _plan_kernel,
        in_specs=[smem, smem, vmem, vmem, vmem],
        out_specs=vmem,
        out_shape=jax.ShapeDtypeStruct((tile.shape[0], ROUTE_TILE), I32),
        compiler_params=pltpu.CompilerParams(vmem_limit_bytes=VMEM_LIMIT),
    )(tile, k, lo, hi, delta)


def _combine_kernel(tile_s, k_s, m_s, nch_s, x1_ref, rows_ref, g_ref, y_ref, acc_ref):
    c = pl.program_id(0)
    t = tile_s[c]
    k = k_s[c]
    W = rows_ref.shape[0]
    CT, D = x1_ref.shape

    @pl.when(k == 0)
    def _():
        acc_ref[...] = x1_ref[...]

    rows = _unpack_rows(rows_ref[:, :D // 2])
    side_t = rows_ref[:, D // 2:].astype(F32).T
    tok = side_t[0:1, :]
    q = k * W + lax.broadcasted_iota(I32, (1, W), 1)
    hit = (tok == lax.broadcasted_iota(I32, (CT, W), 0).astype(F32)) & (q < m_s[t])
    onehot = jnp.where(hit, 1.0, 0.0).astype(BF16)
    acc_ref[...] += _dot(onehot, rows)

    @pl.when(k == nch_s[t] - 1)
    def _():
        y_ref[...] = _rms(acc_ref[...], g_ref[...])


def _combine(tile, k, m, nch, x1, rows, g):
    T, D = x1.shape
    W = ROUTE_TILE
    CT = COMBINE_TILE
    RW = rows.shape[1]
    grid_spec = pltpu.PrefetchScalarGridSpec(
        num_scalar_prefetch=4,
        grid=(tile.shape[0],),
        in_specs=[pl.BlockSpec((CT, D), lambda c, tile_s, *_: (tile_s[c], 0)),
                  pl.BlockSpec((W, RW), lambda c, *_: (c, 0)),
                  pl.BlockSpec((1, D), lambda c, *_: (0, 0))],
        out_specs=pl.BlockSpec((CT, D), lambda c, tile_s, *_: (tile_s[c], 0)),
        scratch_shapes=[pltpu.VMEM((CT, D), F32)],
    )
    return pl.pallas_call(
        _combine_kernel,
        grid_spec=grid_spec,
        out_shape=jax.ShapeDtypeStruct((T, D), F32),
        compiler_params=_cparams(("arbitrary",)),
    )(tile, k, m, nch, x1, rows, g)


def _layer(x, p, spec_cache):
    B, L, D = x.shape
    T = B * L
    da = p["conv_dw_w"].shape[1]
    db = p["hy_skip"].shape[1]
    E = p["n_experts"]
    W = ROUTE_TILE
    cap = max(1, min(T, CAPACITY_FACTOR * T // E))
    assert T % COMBINE_TILE == 0 and cap % W == 0 and L % CONV_ROWS == 0 and E < SIDE_LANES

    if L not in spec_cache:
        wf, wi = _dft_tables(L)
        kre, kim = _filter_spectrum(L, p["hy_f_w1"], p["hy_f_b1"], p["hy_f_w2"], p["hy_f_b2"], p["hy_f_w3"],
                                    p["hy_f_b3"], p["hy_f_w4"], p["hy_f_freq"], wf)
        spec_cache[L] = (wf, wi, kre, kim)
    wf, wi, kre, kim = spec_cache[L]

    x2 = x.reshape(T, D)
    a, hy, gate = _inproj(x2, p["norm1_g"], p["w_in"], p["b_in"], da, db)
    act, x0, vv = _convs(a.reshape(B, L, da), hy.reshape(B, L, 3 * db), p["conv_dw_w"], p["conv_dw_b"],
                         p["conv_ln_g"], p["conv_ln_b"], p["hy_short_w"], p["hy_short_b"])
    u = _long_conv(vv, x0, wf, wi, kre, kim, p["hy_skip"])
    x1, tok = _post(x2, act.reshape(T, da), u.reshape(T, db), gate, p["conv_w_out"], p["conv_b_out"],
                    p["hy_w_out"], p["hy_b_out"], p["w_o"], p["norm2_g"], p["router_w"], E)
    nt = T // W
    aff = lax.bitcast_convert_type(tok[:, D // 2 + 1:D // 2 + 1 + E], F32).T
    idx, ntab, otab = _select(aff.reshape(E, nt, W), cap)
    xs = _sc_gather(tok, idx.reshape(E * cap))
    ye = _expert_ffn(xs, p["exp_w_gate"], p["exp_w_up"], p["exp_w_down"])
    tile, k, m, nch, lo, hi, delta = _combine_tables(ntab[:, :, 0].astype(I32), otab[:, :, 0].astype(I32), cap)
    src = _combine_plan(tile, k, lo, hi, delta)
    rows = _sc_gather(ye, src.reshape(-1))
    y = _combine(tile, k, m, nch, x1, rows, p["norm_f_g"])
    return y.reshape(B, L, D)


def kernel(x_prompt, x_sample, norm1_g, w_in, b_in, conv_dw_w, conv_dw_b, conv_ln_g, conv_ln_b, conv_w_out, conv_b_out, hy_short_w, hy_short_b, hy_f_w1, hy_f_b1, hy_f_w2, hy_f_b2, hy_f_w3, hy_f_b3, hy_f_w4, hy_f_freq, hy_skip, hy_w_out, hy_b_out, w_o, norm2_g, router_w, exp_w_gate, exp_w_up, exp_w_down, norm_f_g):
    depth = w_in.shape[0]
    assert depth == 1
    i = 0
    row = lambda a: a[i][None].astype(F32)
    p = {
        "norm1_g": row(norm1_g), "w_in": w_in[i].astype(BF16), "b_in": row(b_in),
        "conv_dw_w": conv_dw_w[i].astype(F32), "conv_dw_b": row(conv_dw_b),
        "conv_ln_g": row(conv_ln_g), "conv_ln_b": row(conv_ln_b),
        "conv_w_out": conv_w_out[i].astype(BF16), "conv_b_out": row(conv_b_out),
        "hy_short_w": hy_short_w[i].astype(F32), "hy_short_b": row(hy_short_b),
        "hy_f_w1": hy_f_w1[i], "hy_f_b1": hy_f_b1[i], "hy_f_w2": hy_f_w2[i], "hy_f_b2": hy_f_b2[i],
        "hy_f_w3": hy_f_w3[i], "hy_f_b3": hy_f_b3[i], "hy_f_w4": hy_f_w4[i], "hy_f_freq": hy_f_freq[i],
        "hy_skip": row(hy_skip), "hy_w_out": hy_w_out[i].astype(BF16), "hy_b_out": row(hy_b_out),
        "w_o": w_o[i].astype(BF16), "norm2_g": row(norm2_g), "n_experts": router_w.shape[2],
        "router_w": jnp.pad(router_w[i], ((0, 0), (1, SIDE_LANES - 1 - router_w.shape[2]))).astype(BF16),
        "exp_w_gate": exp_w_gate[i].astype(BF16), "exp_w_up": exp_w_up[i].astype(BF16),
        "exp_w_down": exp_w_down[i].astype(BF16), "norm_f_g": norm_f_g[None].astype(F32),
    }
    spec_cache = {}
    return (_layer(x_prompt, p, spec_cache), _layer(x_sample, p, spec_cache))
```

```python
import functools
import math

import jax
import jax.numpy as jnp
from jax import lax
from jax.experimental import pallas as pl
from jax.experimental.pallas import tpu as pltpu
from jax.experimental.pallas import tpu_sc as plsc

F32 = jnp.float32
BF16 = jnp.bfloat16
I32 = jnp.int32

NORM_EPS = 1e-6
LN_EPS = 1e-5
FAST_DECAY_PCT = 0.3
SLOW_DECAY_PCT = 1.5
DECAY_TARGET = 1e-2
CAPACITY_FACTOR = 2

ROUTE_TILE = 256
TOKEN_TILE = 512
CONV_ROWS = 128
CONV_PAD = 16
FFN_ROWS = 1024
COMBINE_TILE = 512
COMBINE_ROWS = 512
SIDE_LANES = 128
VMEM_LIMIT = 56 * 1024 * 1024


def _cparams(sem):
    return pltpu.CompilerParams(dimension_semantics=sem, vmem_limit_bytes=VMEM_LIMIT)


def _dot(a, b):
    return jnp.dot(a, b, preferred_element_type=F32)


def _dot_nt(a, b, precision=None):
    return lax.dot_general(a, b, (((1,), (1,)), ((), ())), preferred_element_type=F32, precision=precision)


def _rms(x, g):
    return x * lax.rsqrt(jnp.mean(x * x, axis=-1, keepdims=True) + NORM_EPS) * g


_HI_MASK = -65536


def _pack_rows(xb):
    half = xb.shape[1] // 2
    lo = lax.bitcast_convert_type(xb[:, :half].astype(F32), I32)
    hi = lax.bitcast_convert_type(xb[:, half:].astype(F32), I32)
    return (hi & _HI_MASK) | lax.shift_right_logical(lo, jnp.full(lo.shape, 16, I32))


def _unpack_rows(w):
    lo = lax.bitcast_convert_type(lax.shift_left(w, jnp.full(w.shape, 16, I32)), F32)
    hi = lax.bitcast_convert_type(w & _HI_MASK, F32)
    return jnp.concatenate([lo, hi], axis=1).astype(BF16)


def _filter_kernel(z_ref, w1_ref, b1_ref, w2_ref, b2_ref, w3_ref, b3_ref, w4_ref, fr_ref, dl_ref, sd_ref):
    hp = lax.Precision.HIGHEST
    fr = fr_ref[...]
    z = z_ref[...]
    h = jnp.sin(fr * (jnp.dot(z, w1_ref[...], precision=hp, preferred_element_type=F32) + b1_ref[...]))
    h = jnp.sin(fr * (jnp.dot(h, w2_ref[...], precision=hp, preferred_element_type=F32) + b2_ref[...]))
    h = jnp.sin(fr * (jnp.dot(h, w3_ref[...], precision=hp, preferred_element_type=F32) + b3_ref[...]))
    h4 = jnp.dot(h, w4_ref[...], precision=hp, preferred_element_type=F32)
    L = z.shape[0]
    db = dl_ref.shape[1]
    t = z[:, 0:1]
    decay = jnp.exp(-t * dl_ref[...])
    hf = h4[:, :db] * decay
    hb = h4[:, db:] * decay
    norm = jnp.sum(jnp.abs(hf), axis=0, keepdims=True) + jnp.sum(jnp.abs(hb), axis=0, keepdims=True)
    hf = hf / norm
    hb = hb / norm
    row = lax.broadcasted_iota(I32, (L, 1), 0)
    hbs = jnp.where(row == 0, 0.0, pltpu.roll(hb, 1, axis=0))
    s = hf + hbs
    d = hf - hbs
    s_hi = s.astype(BF16)
    d_hi = d.astype(BF16)
    sd_ref[:, 0 * db:1 * db] = s_hi
    sd_ref[:, 1 * db:2 * db] = (s - s_hi.astype(F32)).astype(BF16)
    sd_ref[:, 2 * db:3 * db] = d_hi
    sd_ref[:, 3 * db:4 * db] = (d - d_hi.astype(F32)).astype(BF16)


def _spec_kernel(wf_ref, sd_ref, ka_ref, kb_ref):
    db = ka_ref.shape[1]
    ks = _dot(wf_ref[...], sd_ref[...])
    ka_ref[...] = ks[:, 0:db] + ks[:, db:2 * db]
    kb_ref[...] = ks[:, 2 * db:3 * db] + ks[:, 3 * db:4 * db]


def _dft_tables(L):
    n2 = 2 * L
    k = jnp.arange(L, dtype=I32)[:, None]
    n = jnp.arange(L, dtype=I32)[None, :]
    ph = ((k * n) % n2).astype(F32) * (2.0 * math.pi / n2)
    c = jnp.cos(ph)
    s = jnp.sin(ph)
    alt = jnp.where(n % 2 == 0, 1.0, -1.0).astype(F32)
    wf_im = jnp.where(k == 0, alt, -s)
    wf = jnp.concatenate([c, wf_im], axis=0).astype(BF16)
    ck = jnp.where(k == 0, 1.0, 2.0) / n2
    wi_re = (c * ck).T
    wi_im = jnp.where(k == 0, alt / n2, -s * ck).T
    wi = jnp.concatenate([wi_re, wi_im], axis=1).astype(BF16)
    return wf, wi


def _filter_spectrum(L, w1, b1, w2, b2, w3, b3, w4, freq, wf):
    pe, ffn = w1.shape
    db = w4.shape[1] // 2
    bands = (pe - 1) // 2
    t = jnp.linspace(0.0, 1.0, L, dtype=F32)[:, None]
    w = 2.0 * math.pi * jnp.arange(L, dtype=F32)[:, None] / L
    bnd = jnp.linspace(1e-4, bands - 1, bands, dtype=F32)[None, :]
    z = jnp.concatenate([t, jnp.cos(bnd * w), -jnp.sin(bnd * w)], axis=-1)
    P = 128
    zp = jnp.zeros((L, P), F32).at[:, :pe].set(z)
    pad2 = lambda a, r, c: jnp.zeros((r, c), F32).at[:a.shape[0], :a.shape[1]].set(a.astype(F32))
    max_decay = math.log(DECAY_TARGET) / FAST_DECAY_PCT
    min_decay = math.log(DECAY_TARGET) / SLOW_DECAY_PCT
    deltas = jnp.abs(jnp.linspace(min_decay, max_decay, db, dtype=F32))[None, :]
    sd = pl.pallas_call(
        _filter_kernel,
        out_shape=jax.ShapeDtypeStruct((L, 4 * db), BF16),
        compiler_params=pltpu.CompilerParams(vmem_limit_bytes=VMEM_LIMIT),
    )(zp, pad2(w1, P, P), pad2(b1[None], 1, P), pad2(w2, P, P), pad2(b2[None], 1, P), pad2(w3, P, P),
      pad2(b3[None], 1, P), pad2(w4, P, 2 * db), pad2(freq[None], 1, P), deltas)
    rb = min(512, L)
    ka, kb = pl.pallas_call(
        _spec_kernel,
        grid=(2 * L // rb,),
        in_specs=[pl.BlockSpec((rb, L), lambda i: (i, 0)), pl.BlockSpec((L, 4 * db), lambda i: (0, 0))],
        out_specs=[pl.BlockSpec((rb, db), lambda i: (i, 0)), pl.BlockSpec((rb, db), lambda i: (i, 0))],
        out_shape=[jax.ShapeDtypeStruct((2 * L, db), F32)] * 2,
        compiler_params=_cparams(("parallel",)),
    )(wf, sd)
    kre = ka[:L]
    kim = jnp.concatenate([ka[L:L + 1], kb[L + 1:]], axis=0)
    return kre, kim


def _inproj_kernel(x_ref, g_ref, w_ref, b_ref, a_ref, hy_ref, gate_ref, *, da, db):
    h = _rms(x_ref[...], g_ref[...]).astype(BF16)
    c0, c1, c2 = da, 2 * da, 2 * da + 3 * db
    za = _dot(h, w_ref[:, 0:c0]) + b_ref[:, 0:c0]
    zg = _dot(h, w_ref[:, c0:c1]) + b_ref[:, c0:c1]
    a_ref[...] = (za * jax.nn.sigmoid(zg)).astype(BF16)
    hy_ref[...] = (_dot(h, w_ref[:, c1:c2]) + b_ref[:, c1:c2]).astype(BF16)
    gate_ref[...] = jax.nn.sigmoid(_dot(h, w_ref[:, c2:]) + b_ref[:, c2:]).astype(BF16)


def _inproj(x, g, w, b, da, db):
    T, D = x.shape
    C = w.shape[1]
    tm = min(TOKEN_TILE, T)
    ng = C - 2 * da - 3 * db
    row = lambda i: (i, 0)
    fix = lambda i: (0, 0)
    return pl.pallas_call(
        functools.partial(_inproj_kernel, da=da, db=db),
        grid=(T // tm,),
        in_specs=[pl.BlockSpec((tm, D), row), pl.BlockSpec((1, D), fix), pl.BlockSpec((D, C), fix),
                  pl.BlockSpec((1, C), fix)],
        out_specs=[pl.BlockSpec((tm, da), row), pl.BlockSpec((tm, 3 * db), row), pl.BlockSpec((tm, ng), row)],
        out_shape=[jax.ShapeDtypeStruct((T, da), BF16), jax.ShapeDtypeStruct((T, 3 * db), BF16),
                   jax.ShapeDtypeStruct((T, ng), BF16)],
        compiler_params=_cparams(("parallel",)),
    )(x, g, w, b)


def _conv_kernel(a_ref, hy_ref, cw_ref, cb_ref, lg_ref, lb_ref, sw_ref, sb_ref,
                 act_ref, x0_ref, vv_ref, pa_ref, ph_ref, sh_ref, cv_ref, *, L, da, db):
    K = cw_ref.shape[0]
    KS = sw_ref.shape[0]
    R = CONV_ROWS
    P = CONV_PAD
    LANES = 128
    zeros_a = jnp.zeros((P, da), F32)
    zeros_h = jnp.zeros((P, 3 * db), F32)
    pa_ref[0:P, :] = zeros_a
    pa_ref[P + L:P + L + P, :] = zeros_a
    ph_ref[0:P, :] = zeros_h
    ph_ref[P + L:P + L + P, :] = zeros_h
    pa_ref[P:P + L, :] = a_ref[0].astype(F32)
    ph_ref[P:P + L, :] = hy_ref[0].astype(F32)

    def chunk(c, carry):
        base = pl.multiple_of(c * R, R)
        for cb in range(da // LANES):
            cols = slice(cb * LANES, (cb + 1) * LANES)
            win = pa_ref[pl.ds(base, R + 2 * P), cols]
            for r in range(8):
                sh_ref[r] = win[r:r + R + 2 * P - 8, :]
            acc = jnp.zeros((R, LANES), F32) + cb_ref[:, cols]
            for k in range(K):
                q, r = divmod(P - K // 2 + k, 8)
                acc = acc + cw_ref[k:k + 1, cols] * sh_ref[r, 8 * q:8 * q + R, :]
            cv_ref[:, cols] = acc
        acc = cv_ref[...]
        mu = jnp.mean(acc, axis=-1, keepdims=True)
        xc = acc - mu
        var = jnp.mean(xc * xc, axis=-1, keepdims=True)
        y = xc * lax.rsqrt(var + LN_EPS) * lg_ref[...] + lb_ref[...]
        act_ref[0, pl.ds(base, R), :] = (y * jax.nn.sigmoid(y)).astype(BF16)

        def short(lo):
            cols = slice(lo, lo + LANES)
            win = ph_ref[pl.ds(base, R + 2 * P), cols]
            u = jnp.zeros((R, LANES), F32) + sb_ref[:, cols]
            for k in range(KS):
                o = P - KS // 2 + k
                u = u + sw_ref[k:k + 1, cols] * win[o:o + R, :]
            return u

        for cb in range(db // LANES):
            lo = cb * LANES
            cols = slice(lo, lo + LANES)
            vv_ref[0, pl.ds(base, R), cols] = (short(db + lo) * short(2 * db + lo)).astype(BF16)
            x0_ref[0, pl.ds(base, R), cols] = short(lo).astype(BF16)
        return carry

    lax.fori_loop(0, L // R, chunk, 0)


def _convs(a, hy, cw, cb, lg, lb, sw, sb):
    B, L, da = a.shape
    db = hy.shape[2] // 3
    bat = lambda b: (b, 0, 0)
    fix = lambda b: (0, 0)
    return pl.pallas_call(
        functools.partial(_conv_kernel, L=L, da=da, db=db),
        grid=(B,),
        in_specs=[pl.BlockSpec((1, L, da), bat), pl.BlockSpec((1, L, 3 * db), bat),
                  pl.BlockSpec(cw.shape, fix), pl.BlockSpec((1, da), fix), pl.BlockSpec((1, da), fix),
                  pl.BlockSpec((1, da), fix), pl.BlockSpec(sw.shape, fix), pl.BlockSpec((1, 3 * db), fix)],
        out_specs=[pl.BlockSpec((1, L, da), bat), pl.BlockSpec((1, L, db), bat), pl.BlockSpec((1, L, db), bat)],
        out_shape=[jax.ShapeDtypeStruct((B, L, da), BF16), jax.ShapeDtypeStruct((B, L, db), BF16),
                   jax.ShapeDtypeStruct((B, L, db), BF16)],
        scratch_shapes=[pltpu.VMEM((L + 2 * CONV_PAD, da), F32), pltpu.VMEM((L + 2 * CONV_PAD, 3 * db), F32),
                        pltpu.VMEM((8, CONV_ROWS + 2 * CONV_PAD - 8, 128), F32), pltpu.VMEM((CONV_ROWS, da), F32)],
        compiler_params=_cparams(("parallel",)),
    )(a, hy, cw, cb, lg, lb, sw, sb)


def _lconv_kernel(v_ref, x0_ref, wfr_ref, wfi_ref, kre_ref, kim_ref, wir_ref, wii_ref, skip_ref,
                  u_ref, acc_ref, *, nfb, fb):
    j = pl.program_id(1)
    v = v_ref[0]
    xre = _dot(wfr_ref[...], v)
    xim = _dot(wfi_ref[...], v)
    kre = kre_ref[...]
    kim = kim_ref[...]
    row0 = (lax.broadcasted_iota(I32, (fb, 1), 0) + j * fb) == 0
    t = xim * kim
    pre = xre * kre - jnp.where(row0, 0.0, t)
    pim = jnp.where(row0, t, xre * kim + xim * kre)
    contrib = _dot(wir_ref[...], pre.astype(BF16)) + _dot(wii_ref[...], pim.astype(BF16))

    @pl.when(j == 0)
    def _():
        acc_ref[...] = contrib

    @pl.when(j > 0)
    def _():
        acc_ref[...] += contrib

    @pl.when(j == nfb - 1)
    def _():
        v2 = acc_ref[...] + skip_ref[...] * v.astype(F32)
        u_ref[0] = (x0_ref[0].astype(F32) * v2).astype(BF16)


def _long_conv(vv, x0, wf, wi, kre, kim, skip):
    B, L, db = vv.shape
    fb = min(512, L)
    nfb = L // fb
    bat = lambda b, j: (b, 0, 0)
    return pl.pallas_call(
        functools.partial(_lconv_kernel, nfb=nfb, fb=fb),
        grid=(B, nfb),
        in_specs=[pl.BlockSpec((1, L, db), bat), pl.BlockSpec((1, L, db), bat),
                  pl.BlockSpec((fb, L), lambda b, j: (j, 0)), pl.BlockSpec((fb, L), lambda b, j: (nfb + j, 0)),
                  pl.BlockSpec((fb, db), lambda b, j: (j, 0)), pl.BlockSpec((fb, db), lambda b, j: (j, 0)),
                  pl.BlockSpec((L, fb), lambda b, j: (0, j)), pl.BlockSpec((L, fb), lambda b, j: (0, nfb + j)),
                  pl.BlockSpec((1, db), lambda b, j: (0, 0))],
        out_specs=pl.BlockSpec((1, L, db), bat),
        out_shape=jax.ShapeDtypeStruct((B, L, db), BF16),
        scratch_shapes=[pltpu.VMEM((L, db), F32)],
        compiler_params=_cparams(("parallel", "arbitrary")),
    )(vv, x0, wf, wf, kre, kim, wi, wi, skip)


def _post_kernel(x_ref, act_ref, u_ref, gate_ref, wa_ref, ba_ref, wb_ref, bb_ref, wo_ref, n2_ref, rw_ref,
                 x1_ref, tok_ref, *, E):
    tm, D = x_ref.shape
    ya = _dot(act_ref[...], wa_ref[...]) + ba_ref[...]
    yb = _dot(u_ref[...], wb_ref[...]) + bb_ref[...]
    g = gate_ref[...].astype(F32)
    m = g[:, :D] * ya + g[:, D:] * yb
    x1 = x_ref[...] + _dot(m.astype(BF16), wo_ref[...])
    x1_ref[...] = x1
    h2 = _rms(x1, n2_ref[...]).astype(BF16)
    tok_ref[:, :D // 2] = _pack_rows(h2)
    lane = lax.broadcasted_iota(I32, (tm, SIDE_LANES), 1)
    is_expert = (lane >= 1) & (lane <= E)
    logits = jnp.where(is_expert, _dot(h2, rw_ref[...]), -1e30)
    ex = jnp.exp(logits - jnp.max(logits, axis=1, keepdims=True))
    aff = jnp.where(is_expert, ex / jnp.sum(ex, axis=1, keepdims=True), 0.0)
    tok_id = (pl.program_id(0) * tm) % COMBINE_TILE + lax.broadcasted_iota(I32, (tm, SIDE_LANES), 0)
    tok_ref[:, D // 2:] = jnp.where(lane == 0, tok_id, lax.bitcast_convert_type(aff, I32))


def _post(x, act, u, gate, wa, ba, wb, bb, wo, n2, rw, E):
    T, D = x.shape
    tm = min(TOKEN_TILE, T)
    assert COMBINE_TILE % tm == 0
    RW = D // 2 + SIDE_LANES
    row = lambda i: (i, 0)
    fix = lambda i: (0, 0)
    full = lambda a: pl.BlockSpec(a.shape, fix)
    return pl.pallas_call(
        functools.partial(_post_kernel, E=E),
        grid=(T // tm,),
        in_specs=[pl.BlockSpec((tm, D), row), pl.BlockSpec((tm, act.shape[1]), row),
                  pl.BlockSpec((tm, u.shape[1]), row), pl.BlockSpec((tm, gate.shape[1]), row),
                  full(wa), full(ba), full(wb), full(bb), full(wo), full(n2), full(rw)],
        out_specs=[pl.BlockSpec((tm, D), row), pl.BlockSpec((tm, RW), row)],
        out_shape=[jax.ShapeDtypeStruct((T, D), F32), jax.ShapeDtypeStruct((T, RW), I32)],
        compiler_params=_cparams(("parallel",)),
    )(x, act, u, gate, wa, ba, wb, bb, wo, n2, rw)


def _select_kernel(a_ref, idx_ref, ntab_ref, otab_ref, thr_ref, *, cap):
    E, nt, W = a_ref.shape
    nsb = cap // W
    bits = lax.bitcast_convert_type(a_ref[...], I32)

    def bisect(b, cur):
        cand = cur | jnp.left_shift(jnp.int32(1), 30 - b)
        hit = jnp.where(bits >= cand, 1.0, 0.0)
        cnt = jnp.sum(jnp.sum(hit, axis=2, keepdims=True), axis=1, keepdims=True)
        return jnp.where(cnt >= cap, cand, cur)

    thr_ref[...] = lax.fori_loop(0, 31, bisect, jnp.zeros((E, 1, 1), I32))

    r_i = lax.broadcasted_iota(I32, (W, W), 0)
    c_i = lax.broadcasted_iota(I32, (W, W), 1)
    upper = jnp.where(r_i <= c_i, 1.0, 0.0).astype(BF16)
    ones = jnp.ones((W, W), BF16)
    tr = lax.broadcasted_iota(I32, (nt, nt), 0)
    tc = lax.broadcasted_iota(I32, (nt, nt), 1)
    lower = jnp.where(tc < tr, 1.0, 0.0).astype(BF16)
    tile_id = lax.broadcasted_iota(I32, (nt, W), 0).astype(F32)
    lane = lax.broadcasted_iota(I32, (1, W), 1).astype(F32)

    def prefix(maskf):
        mb = maskf.astype(BF16)
        incl = _dot(mb, upper)
        tot = _dot(mb, ones)
        off = _dot(lower, tot.astype(BF16))
        return incl, tot, off

    def expert(e, carry):
        ae = a_ref[e]
        be = lax.bitcast_convert_type(ae, I32)
        th = thr_ref[e]
        gt = be > th
        eq = be == th
        gtf = jnp.where(gt, 1.0, 0.0)
        eqf = jnp.where(eq, 1.0, 0.0)
        need = cap - jnp.sum(jnp.sum(gtf, axis=1, keepdims=True), axis=0, keepdims=True)
        e_incl, _, e_off = prefix(eqf)
        eq_rank = e_off + e_incl - eqf
        sel = gt | (eq & (eq_rank < need))
        self_ = jnp.where(sel, 1.0, 0.0)
        incl, tot, off = prefix(self_)
        ntab_ref[e] = tot[:, 0:128]
        otab_ref[e] = off[:, 0:128]
        incl_t = incl.T.astype(BF16)
        off_hi = off + tot

        def slots(c, carry2):
            s_row = lane + jnp.asarray(c * W, F32)
            oh = jnp.where((off <= s_row) & (s_row < off_hi), 1.0, 0.0)
            tile = jnp.sum(oh * tile_id, axis=0, keepdims=True)
            rank = s_row - jnp.sum(oh * off, axis=0, keepdims=True)
            rt = _dot(incl_t, oh.astype(BF16))
            pos = jnp.sum(jnp.where(rt <= rank, 1.0, 0.0), axis=0, keepdims=True)
            idx_ref[e, pl.ds(c, 1), :] = (tile * W + pos).astype(I32)
            return carry2

        lax.fori_loop(0, nsb, slots, 0)
        return carry

    lax.fori_loop(0, E, expert, 0)


def _select(aff3, cap):
    E, nt, W = aff3.shape
    return pl.pallas_call(
        functools.partial(_select_kernel, cap=cap),
        out_shape=[jax.ShapeDtypeStruct((E, cap // W, W), I32), jax.ShapeDtypeStruct((E, nt, 128), F32),
                   jax.ShapeDtypeStruct((E, nt, 128), F32)],
        scratch_shapes=[pltpu.VMEM((E, 1, 1), I32)],
        compiler_params=pltpu.CompilerParams(vmem_limit_bytes=VMEM_LIMIT),
    )(aff3)


SC_GATHER_ROWS = 64


def _sc_gather(table, idx):
    M = idx.shape[0]
    Dw = table.shape[1]
    win = SC_GATHER_ROWS
    info = plsc.get_sparse_core_info()
    nc = info.num_cores
    nw = nc * info.num_subcores
    per_w = M // nw
    assert M % (nw * win) == 0
    mesh = plsc.VectorSubcoreMesh(core_axis_name="c", subcore_axis_name="s")

    @functools.partial(
        pl.kernel, mesh=mesh,
        out_type=jax.ShapeDtypeStruct((M, Dw), table.dtype),
        scratch_types=[pltpu.VMEM((win,), I32), pltpu.VMEM((win, Dw), table.dtype), pltpu.SemaphoreType.DMA],
    )
    def gather(table_hbm, idx_hbm, out_hbm, idx_v, rows_v, sem):
        wid = lax.axis_index("s") * nc + lax.axis_index("c")

        @pl.loop(0, per_w // win)
        def _(j):
            base = pl.multiple_of(wid * per_w + j * win, win)
            pltpu.sync_copy(idx_hbm.at[pl.ds(base, win)], idx_v)
            pltpu.async_copy(table_hbm.at[idx_v], rows_v, sem).wait()
            pltpu.sync_copy(rows_v, out_hbm.at[pl.ds(base, win)])

    return gather(table, idx)


def _ffn_kernel(xs_ref, wg_ref, wu_ref, wd_ref, ye_ref):
    R = xs_ref.shape[0]
    half = wg_ref.shape[1] // 2
    xs = _unpack_rows(xs_ref[:, :half])
    side = xs_ref[:, half:]
    lane = lax.broadcasted_iota(I32, (R, SIDE_LANES), 1)
    aff = lax.bitcast_convert_type(side, F32)
    gate = jnp.sum(jnp.where(lane == pl.program_id(0) + 1, aff, 0.0), axis=1, keepdims=True)
    g = _dot(xs, wg_ref[0])
    u = _dot(xs, wu_ref[0])
    hid = (g * jax.nn.sigmoid(g) * u).astype(BF16)
    ye = _dot(hid, wd_ref[0]) * gate
    ye_ref[:, :half] = _pack_rows(ye.astype(BF16))
    ye_ref[:, half:] = side


def _expert_ffn(xs, wg, wu, wd):
    E, D, F = wg.shape
    cap = xs.shape[0] // E
    R = min(FFN_ROWS, cap)
    assert cap % R == 0
    RW = xs.shape[1]
    nsb = cap // R
    return pl.pallas_call(
        _ffn_kernel,
        grid=(E, nsb),
        in_specs=[pl.BlockSpec((R, RW), lambda e, s: (e * nsb + s, 0)),
                  pl.BlockSpec((1, D, F), lambda e, s: (e, 0, 0)), pl.BlockSpec((1, D, F), lambda e, s: (e, 0, 0)),
                  pl.BlockSpec((1, F, D), lambda e, s: (e, 0, 0))],
        out_specs=pl.BlockSpec((R, RW), lambda e, s: (e * nsb + s, 0)),
        out_shape=jax.ShapeDtypeStruct(xs.shape, I32),
        compiler_params=_cparams(("parallel", "parallel")),
    )(xs, wg, wu, wd)


def _combine_tables(ntab, otab, cap):
    E, nt = ntab.shape
    W = COMBINE_ROWS
    g = COMBINE_TILE // ROUTE_TILE
    ntc = nt // g
    n2 = ntab.reshape(E, ntc, g).sum(-1)
    off2 = otab.reshape(E, ntc, g)[:, :, 0]
    hi = jnp.cumsum(n2, axis=0)
    lo = hi - n2
    m = hi[-1]
    nch = jnp.maximum(1, (m + W - 1) // W)
    cb_hi = jnp.cumsum(nch)
    cb_lo = cb_hi - nch
    nc_max = (E * cap) // W + ntc
    c = jnp.arange(nc_max, dtype=I32)
    tile = jnp.minimum(jnp.sum((cb_hi[None, :] <= c[:, None]).astype(I32), axis=1), ntc - 1)
    k = jnp.where(c < cb_hi[-1], c - cb_lo[tile], nch[ntc - 1] + c - cb_hi[-1])
    delta = jnp.arange(E, dtype=I32)[:, None] * cap + off2 - lo
    bcast = lambda a: jnp.broadcast_to(a.T.astype(F32)[:, :, None], (ntc, E, 128))
    return tile, k.astype(I32), m.astype(I32), nch.astype(I32), bcast(lo), bcast(hi), bcast(delta)


def _plan_kernel(tile_s, k_s, lo_ref, hi_ref, dl_ref, src_ref, *, n_rows):
    W = src_ref.shape[1]
    lane = lax.broadcasted_iota(I32, (1, W), 1).astype(F32)

    def chunk(c, carry):
        t = tile_s[c]
        q = lane + jnp.asarray(k_s[c] * W, F32)
        lo = lo_ref[t][:, 0:1]
        hi = hi_ref[t][:, 0:1]
        mine = (lo <= q) & (q < hi)
        src = jnp.sum(jnp.where(mine, dl_ref[t][:, 0:1] + q, 0.0), axis=0, keepdims=True)
        used = jnp.sum(jnp.where(mine, 1.0, 0.0), axis=0, keepdims=True) > 0.0
        p = lane + jnp.asarray(c * W, F32)
        spare = jnp.where(p < n_rows, p, p - n_rows)
        src_ref[pl.ds(c, 1), :] = jnp.where(used, src, spare).astype(I32)
        return carry

    lax.fori_loop(0, src_ref.shape[0], chunk, 0)


def _combine_plan(tile, k, lo, hi, delta, n_rows):
    smem = pl.BlockSpec(memory_space=pltpu.SMEM)
    vmem = pl.BlockSpec(memory_space=pltpu.VMEM)
    assert tile.shape[0] * COMBINE_ROWS <= 2 * n_rows
    return pl.pallas_call(
        functools.partial(_plan_kernel, n_rows=n_rows),
        in_specs=[smem, smem, vmem, vmem, vmem],
        out_specs=vmem,
        out_shape=jax.ShapeDtypeStruct((tile.shape[0], COMBINE_ROWS), I32),
        compiler_params=pltpu.CompilerParams(vmem_limit_bytes=VMEM_LIMIT),
    )(tile, k, lo, hi, delta)


def _combine_kernel(tile_s, k_s, m_s, nch_s, x1_ref, rows_ref, g_ref, y_ref, acc_ref):
    c = pl.program_id(0)
    t = tile_s[c]
    k = k_s[c]
    W = rows_ref.shape[0]
    CT, D = x1_ref.shape

    @pl.when(k == 0)
    def _():
        acc_ref[...] = x1_ref[...]

    rows = _unpack_rows(rows_ref[:, :D // 2])
    side_t = rows_ref[:, D // 2:].astype(F32).T
    tok = side_t[0:1, :]
    q = k * W + lax.broadcasted_iota(I32, (1, W), 1)
    hit = (tok == lax.broadcasted_iota(I32, (CT, W), 0).astype(F32)) & (q < m_s[t])
    onehot = jnp.where(hit, 1.0, 0.0).astype(BF16)
    acc_ref[...] += _dot(onehot, rows)

    @pl.when(k == nch_s[t] - 1)
    def _():
        y_ref[...] = _rms(acc_ref[...], g_ref[...])


def _combine(tile, k, m, nch, x1, rows, g):
    T, D = x1.shape
    W = COMBINE_ROWS
    CT = COMBINE_TILE
    RW = rows.shape[1]
    grid_spec = pltpu.PrefetchScalarGridSpec(
        num_scalar_prefetch=4,
        grid=(tile.shape[0],),
        in_specs=[pl.BlockSpec((CT, D), lambda c, tile_s, *_: (tile_s[c], 0)),
                  pl.BlockSpec((W, RW), lambda c, *_: (c, 0)),
                  pl.BlockSpec((1, D), lambda c, *_: (0, 0))],
        out_specs=pl.BlockSpec((CT, D), lambda c, tile_s, *_: (tile_s[c], 0)),
        scratch_shapes=[pltpu.VMEM((CT, D), F32)],
    )
    return pl.pallas_call(
        _combine_kernel,
        grid_spec=grid_spec,
        out_shape=jax.ShapeDtypeStruct((T, D), F32),
        compiler_params=_cparams(("arbitrary",)),
    )(tile, k, m, nch, x1, rows, g)


def _layer(x, p, spec_cache):
    B, L, D = x.shape
    T = B * L
    da = p["conv_dw_w"].shape[1]
    db = p["hy_skip"].shape[1]
    E = p["n_experts"]
    W = ROUTE_TILE
    cap = max(1, min(T, CAPACITY_FACTOR * T // E))
    assert T % COMBINE_TILE == 0 and cap % W == 0 and L % CONV_ROWS == 0 and E < SIDE_LANES

    if L not in spec_cache:
        wf, wi = _dft_tables(L)
        kre, kim = _filter_spectrum(L, p["hy_f_w1"], p["hy_f_b1"], p["hy_f_w2"], p["hy_f_b2"], p["hy_f_w3"],
                                    p["hy_f_b3"], p["hy_f_w4"], p["hy_f_freq"], wf)
        spec_cache[L] = (wf, wi, kre, kim)
    wf, wi, kre, kim = spec_cache[L]

    x2 = x.reshape(T, D)
    a, hy, gate = _inproj(x2, p["norm1_g"], p["w_in"], p["b_in"], da, db)
    act, x0, vv = _convs(a.reshape(B, L, da), hy.reshape(B, L, 3 * db), p["conv_dw_w"], p["conv_dw_b"],
                         p["conv_ln_g"], p["conv_ln_b"], p["hy_short_w"], p["hy_short_b"])
    u = _long_conv(vv, x0, wf, wi, kre, kim, p["hy_skip"])
    x1, tok = _post(x2, act.reshape(T, da), u.reshape(T, db), gate, p["conv_w_out"], p["conv_b_out"],
                    p["hy_w_out"], p["hy_b_out"], p["w_o"], p["norm2_g"], p["router_w"], E)
    nt = T // W
    aff = lax.bitcast_convert_type(tok[:, D // 2 + 1:D // 2 + 1 + E], F32).T
    idx, ntab, otab = _select(aff.reshape(E, nt, W), cap)
    xs = _sc_gather(tok, idx.reshape(E * cap))
    ye = _expert_ffn(xs, p["exp_w_gate"], p["exp_w_up"], p["exp_w_down"])
    tile, k, m, nch, lo, hi, delta = _combine_tables(ntab[:, :, 0].astype(I32), otab[:, :, 0].astype(I32), cap)
    src = _combine_plan(tile, k, lo, hi, delta, E * cap)
    rows = _sc_gather(ye, src.reshape(-1))
    y = _combine(tile, k, m, nch, x1, rows, p["norm_f_g"])
    return y.reshape(B, L, D)


def kernel(x_prompt, x_sample, norm1_g, w_in, b_in, conv_dw_w, conv_dw_b, conv_ln_g, conv_ln_b, conv_w_out, conv_b_out, hy_short_w, hy_short_b, hy_f_w1, hy_f_b1, hy_f_w2, hy_f_b2, hy_f_w3, hy_f_b3, hy_f_w4, hy_f_freq, hy_skip, hy_w_out, hy_b_out, w_o, norm2_g, router_w, exp_w_gate, exp_w_up, exp_w_down, norm_f_g):
    depth = w_in.shape[0]
    assert depth == 1
    i = 0
    row = lambda a: a[i][None].astype(F32)
    p = {
        "norm1_g": row(norm1_g), "w_in": w_in[i].astype(BF16), "b_in": row(b_in),
        "conv_dw_w": conv_dw_w[i].astype(F32), "conv_dw_b": row(conv_dw_b),
        "conv_ln_g": row(conv_ln_g), "conv_ln_b": row(conv_ln_b),
        "conv_w_out": conv_w_out[i].astype(BF16), "conv_b_out": row(conv_b_out),
        "hy_short_w": hy_short_w[i].astype(F32), "hy_short_b": row(hy_short_b),
        "hy_f_w1": hy_f_w1[i], "hy_f_b1": hy_f_b1[i], "hy_f_w2": hy_f_w2[i], "hy_f_b2": hy_f_b2[i],
        "hy_f_w3": hy_f_w3[i], "hy_f_b3": hy_f_b3[i], "hy_f_w4": hy_f_w4[i], "hy_f_freq": hy_f_freq[i],
        "hy_skip": row(hy_skip), "hy_w_out": hy_w_out[i].astype(BF16), "hy_b_out": row(hy_b_out),
        "w_o": w_o[i].astype(BF16), "norm2_g": row(norm2_g), "n_experts": router_w.shape[2],
        "router_w": jnp.pad(router_w[i], ((0, 0), (1, SIDE_LANES - 1 - router_w.shape[2]))).astype(BF16),
        "exp_w_gate": exp_w_gate[i].astype(BF16), "exp_w_up": exp_w_up[i].astype(BF16),
        "exp_w_down": exp_w_down[i].astype(BF16), "norm_f_g": norm_f_g[None].astype(F32),
    }
    spec_cache = {}
    y_sample = _layer(x_sample, p, spec_cache)
    y_prompt = _layer(x_prompt, p, spec_cache)
    return (y_prompt, y_sample)
```

```python
import functools
import math

import jax
import jax.numpy as jnp
from jax import lax
from jax.experimental import pallas as pl
from jax.experimental.pallas import tpu as pltpu
from jax.experimental.pallas import tpu_sc as plsc

F32 = jnp.float32
BF16 = jnp.bfloat16
I32 = jnp.int32

NORM_EPS = 1e-6
LN_EPS = 1e-5
FAST_DECAY_PCT = 0.3
SLOW_DECAY_PCT = 1.5
DECAY_TARGET = 1e-2
CAPACITY_FACTOR = 2

ROUTE_TILE = 256
TOKEN_TILE = 512
CONV_ROWS = 128
CONV_PAD = 16
LCONV_BINS = 512
FFN_ROWS = 1024
COMBINE_TILE = 512
COMBINE_ROWS = 512
SIDE_LANES = 128
VMEM_LIMIT = 56 * 1024 * 1024


def _cparams(sem):
    return pltpu.CompilerParams(dimension_semantics=sem, vmem_limit_bytes=VMEM_LIMIT)


def _dot(a, b):
    return jnp.dot(a, b, preferred_element_type=F32)


def _dot_nt(a, b, precision=None):
    return lax.dot_general(a, b, (((1,), (1,)), ((), ())), preferred_element_type=F32, precision=precision)


def _rms(x, g):
    return x * lax.rsqrt(jnp.mean(x * x, axis=-1, keepdims=True) + NORM_EPS) * g


_HI_MASK = -65536


def _pack_rows(xb):
    half = xb.shape[1] // 2
    lo = lax.bitcast_convert_type(xb[:, :half].astype(F32), I32)
    hi = lax.bitcast_convert_type(xb[:, half:].astype(F32), I32)
    return (hi & _HI_MASK) | lax.shift_right_logical(lo, jnp.full(lo.shape, 16, I32))


def _unpack_rows(w):
    lo = lax.bitcast_convert_type(lax.shift_left(w, jnp.full(w.shape, 16, I32)), F32)
    hi = lax.bitcast_convert_type(w & _HI_MASK, F32)
    return jnp.concatenate([lo, hi], axis=1).astype(BF16)


def _filter_kernel(z_ref, w1_ref, b1_ref, w2_ref, b2_ref, w3_ref, b3_ref, w4_ref, fr_ref, dl_ref, sd_ref):
    hp = lax.Precision.HIGHEST
    fr = fr_ref[...]
    z = z_ref[...]
    h = jnp.sin(fr * (jnp.dot(z, w1_ref[...], precision=hp, preferred_element_type=F32) + b1_ref[...]))
    h = jnp.sin(fr * (jnp.dot(h, w2_ref[...], precision=hp, preferred_element_type=F32) + b2_ref[...]))
    h = jnp.sin(fr * (jnp.dot(h, w3_ref[...], precision=hp, preferred_element_type=F32) + b3_ref[...]))
    h4 = jnp.dot(h, w4_ref[...], precision=hp, preferred_element_type=F32)
    L = z.shape[0]
    db = dl_ref.shape[1]
    t = z[:, 0:1]
    decay = jnp.exp(-t * dl_ref[...])
    hf = h4[:, :db] * decay
    hb = h4[:, db:] * decay
    norm = jnp.sum(jnp.abs(hf), axis=0, keepdims=True) + jnp.sum(jnp.abs(hb), axis=0, keepdims=True)
    hf = hf / norm
    hb = hb / norm
    row = lax.broadcasted_iota(I32, (L, 1), 0)
    hbs = jnp.where(row == 0, 0.0, pltpu.roll(hb, 1, axis=0))
    s = hf + hbs
    d = hf - hbs
    s_hi = s.astype(BF16)
    d_hi = d.astype(BF16)
    sd_ref[:, 0 * db:1 * db] = s_hi
    sd_ref[:, 1 * db:2 * db] = (s - s_hi.astype(F32)).astype(BF16)
    sd_ref[:, 2 * db:3 * db] = d_hi
    sd_ref[:, 3 * db:4 * db] = (d - d_hi.astype(F32)).astype(BF16)


def _split_rows(x):
    w = pltpu.bitcast(x, I32)
    even = lax.bitcast_convert_type(lax.shift_left(w, jnp.full(w.shape, 16, I32)), F32).astype(BF16)
    odd = lax.bitcast_convert_type(w & _HI_MASK, F32).astype(BF16)
    return even, odd


def _merge_rows(even, odd):
    e = lax.bitcast_convert_type(even.astype(BF16).astype(F32), I32)
    o = lax.bitcast_convert_type(odd.astype(BF16).astype(F32), I32)
    return pltpu.bitcast((o & _HI_MASK) | lax.shift_right_logical(e, jnp.full(e.shape, 16, I32)), BF16)


def _half_spectra(xe, xo, wfr, wfi, cw, sw, row0):
    ere = _dot(wfr, xe)
    eim = _dot(wfi, xe)
    ore = _dot(wfr, xo)
    oim = _dot(wfi, xo)
    tre = cw * ore + sw * oim
    tim = cw * oim - sw * ore
    return ere + tre, jnp.where(row0, eim, eim + tim), ere - tre, jnp.where(row0, -oim, tim - eim)


def _first_row(fb, block):
    return (lax.broadcasted_iota(I32, (fb, 1), 0) + block * fb) == 0


def _spec_kernel(sd_ref, wfr_ref, wfi_ref, cw_ref, sw_ref, kare_ref, kaim_ref, kbre_ref, kbim_ref):
    fb, db = kare_ref.shape
    row0 = _first_row(fb, pl.program_id(0))
    xe, xo = _split_rows(sd_ref[...])

    def spectra(g):
        cols = slice(g * db, (g + 1) * db)
        return _half_spectra(xe[:, cols], xo[:, cols], wfr_ref[...], wfi_ref[...], cw_ref[...], sw_ref[...], row0)

    s_hi, s_lo, d_hi, d_lo = spectra(0), spectra(1), spectra(2), spectra(3)
    kare_ref[...] = s_hi[0] + s_lo[0]
    kbre_ref[...] = s_hi[2] + s_lo[2]
    kaim_ref[...] = jnp.where(row0, s_hi[1] + s_lo[1], d_hi[1] + d_lo[1])
    kbim_ref[...] = d_hi[3] + d_lo[3]


def _dft_tables(L):
    n2 = 2 * L
    k = jnp.arange(L, dtype=I32)[:, None]
    n = jnp.arange(L, dtype=I32)[None, :]
    ph = ((k * n) % n2).astype(F32) * (2.0 * math.pi / n2)
    c = jnp.cos(ph)
    s = jnp.sin(ph)
    alt = jnp.where(n % 2 == 0, 1.0, -1.0).astype(F32)
    wf_im = jnp.where(k == 0, alt, -s)
    wf = jnp.concatenate([c, wf_im], axis=0).astype(BF16)
    ck = jnp.where(k == 0, 1.0, 2.0) / n2
    wi_re = (c * ck).T
    wi_im = jnp.where(k == 0, alt / n2, -s * ck).T
    wi = jnp.concatenate([wi_re, wi_im], axis=1).astype(BF16)
    return wf, wi


def _twiddles(L, db):
    ph = jnp.arange(L // 2, dtype=F32)[:, None] * (math.pi / L)
    return jnp.broadcast_to(jnp.cos(ph), (L // 2, db)), jnp.broadcast_to(jnp.sin(ph), (L // 2, db))


def _filter_spectrum(L, w1, b1, w2, b2, w3, b3, w4, freq, wf, cw, sw):
    pe, ffn = w1.shape
    db = w4.shape[1] // 2
    bands = (pe - 1) // 2
    t = jnp.linspace(0.0, 1.0, L, dtype=F32)[:, None]
    w = 2.0 * math.pi * jnp.arange(L, dtype=F32)[:, None] / L
    bnd = jnp.linspace(1e-4, bands - 1, bands, dtype=F32)[None, :]
    z = jnp.concatenate([t, jnp.cos(bnd * w), -jnp.sin(bnd * w)], axis=-1)
    P = 128
    zp = jnp.zeros((L, P), F32).at[:, :pe].set(z)
    pad2 = lambda a, r, c: jnp.zeros((r, c), F32).at[:a.shape[0], :a.shape[1]].set(a.astype(F32))
    max_decay = math.log(DECAY_TARGET) / FAST_DECAY_PCT
    min_decay = math.log(DECAY_TARGET) / SLOW_DECAY_PCT
    deltas = jnp.abs(jnp.linspace(min_decay, max_decay, db, dtype=F32))[None, :]
    sd = pl.pallas_call(
        _filter_kernel,
        out_shape=jax.ShapeDtypeStruct((L, 4 * db), BF16),
        compiler_params=pltpu.CompilerParams(vmem_limit_bytes=VMEM_LIMIT),
    )(zp, pad2(w1, P, P), pad2(b1[None], 1, P), pad2(w2, P, P), pad2(b2[None], 1, P), pad2(w3, P, P),
      pad2(b3[None], 1, P), pad2(w4, P, 2 * db), pad2(freq[None], 1, P), deltas)
    L2 = L // 2
    fb = min(LCONV_BINS, L2)
    nfb = L2 // fb
    blk = pl.BlockSpec((fb, db), lambda j: (j, 0))
    return pl.pallas_call(
        _spec_kernel,
        grid=(nfb,),
        in_specs=[pl.BlockSpec((L, 4 * db), lambda j: (0, 0)),
                  pl.BlockSpec((fb, L2), lambda j: (j, 0)), pl.BlockSpec((fb, L2), lambda j: (nfb + j, 0)), blk, blk],
        out_specs=[blk] * 4,
        out_shape=[jax.ShapeDtypeStruct((L2, db), F32)] * 4,
        compiler_params=_cparams(("parallel",)),
    )(sd, wf, wf, cw, sw)


def _inproj_kernel(x_ref, g_ref, w_ref, b_ref, a_ref, hy_ref, gate_ref, *, da, db):
    h = _rms(x_ref[...], g_ref[...]).astype(BF16)
    c0, c1, c2 = da, 2 * da, 2 * da + 3 * db
    za = _dot(h, w_ref[:, 0:c0]) + b_ref[:, 0:c0]
    zg = _dot(h, w_ref[:, c0:c1]) + b_ref[:, c0:c1]
    a_ref[...] = (za * jax.nn.sigmoid(zg)).astype(BF16)
    hy_ref[...] = (_dot(h, w_ref[:, c1:c2]) + b_ref[:, c1:c2]).astype(BF16)
    gate_ref[...] = jax.nn.sigmoid(_dot(h, w_ref[:, c2:]) + b_ref[:, c2:]).astype(BF16)


def _inproj(x, g, w, b, da, db):
    T, D = x.shape
    C = w.shape[1]
    tm = min(TOKEN_TILE, T)
    ng = C - 2 * da - 3 * db
    row = lambda i: (i, 0)
    fix = lambda i: (0, 0)
    return pl.pallas_call(
        functools.partial(_inproj_kernel, da=da, db=db),
        grid=(T // tm,),
        in_specs=[pl.BlockSpec((tm, D), row), pl.BlockSpec((1, D), fix), pl.BlockSpec((D, C), fix),
                  pl.BlockSpec((1, C), fix)],
        out_specs=[pl.BlockSpec((tm, da), row), pl.BlockSpec((tm, 3 * db), row), pl.BlockSpec((tm, ng), row)],
        out_shape=[jax.ShapeDtypeStruct((T, da), BF16), jax.ShapeDtypeStruct((T, 3 * db), BF16),
                   jax.ShapeDtypeStruct((T, ng), BF16)],
        compiler_params=_cparams(("parallel",)),
    )(x, g, w, b)


def _conv_kernel(a_ref, hy_ref, cw_ref, cb_ref, lg_ref, lb_ref, sw_ref, sb_ref,
                 act_ref, x0_ref, vv_ref, pa_ref, ph_ref, sh_ref, cv_ref, *, L, da, db):
    K = cw_ref.shape[0]
    KS = sw_ref.shape[0]
    R = CONV_ROWS
    P = CONV_PAD
    LANES = 128
    zeros_a = jnp.zeros((P, da), F32)
    zeros_h = jnp.zeros((P, 3 * db), F32)
    pa_ref[0:P, :] = zeros_a
    pa_ref[P + L:P + L + P, :] = zeros_a
    ph_ref[0:P, :] = zeros_h
    ph_ref[P + L:P + L + P, :] = zeros_h
    pa_ref[P:P + L, :] = a_ref[0].astype(F32)
    ph_ref[P:P + L, :] = hy_ref[0].astype(F32)

    def chunk(c, carry):
        base = pl.multiple_of(c * R, R)
        for cb in range(da // LANES):
            cols = slice(cb * LANES, (cb + 1) * LANES)
            win = pa_ref[pl.ds(base, R + 2 * P), cols]
            for r in range(8):
                sh_ref[r] = win[r:r + R + 2 * P - 8, :]
            acc = jnp.zeros((R, LANES), F32) + cb_ref[:, cols]
            for k in range(K):
                q, r = divmod(P - K // 2 + k, 8)
                acc = acc + cw_ref[k:k + 1, cols] * sh_ref[r, 8 * q:8 * q + R, :]
            cv_ref[:, cols] = acc
        acc = cv_ref[...]
        mu = jnp.mean(acc, axis=-1, keepdims=True)
        xc = acc - mu
        var = jnp.mean(xc * xc, axis=-1, keepdims=True)
        y = xc * lax.rsqrt(var + LN_EPS) * lg_ref[...] + lb_ref[...]
        act_ref[0, pl.ds(base, R), :] = (y * jax.nn.sigmoid(y)).astype(BF16)

        def short(lo):
            cols = slice(lo, lo + LANES)
            win = ph_ref[pl.ds(base, R + 2 * P), cols]
            u = jnp.zeros((R, LANES), F32) + sb_ref[:, cols]
            for k in range(KS):
                o = P - KS // 2 + k
                u = u + sw_ref[k:k + 1, cols] * win[o:o + R, :]
            return u

        for cb in range(db // LANES):
            lo = cb * LANES
            cols = slice(lo, lo + LANES)
            vv_ref[0, pl.ds(base, R), cols] = (short(db + lo) * short(2 * db + lo)).astype(BF16)
            x0_ref[0, pl.ds(base, R), cols] = short(lo).astype(BF16)
        return carry

    lax.fori_loop(0, L // R, chunk, 0)


def _convs(a, hy, cw, cb, lg, lb, sw, sb):
    B, L, da = a.shape
    db = hy.shape[2] // 3
    bat = lambda b: (b, 0, 0)
    fix = lambda b: (0, 0)
    return pl.pallas_call(
        functools.partial(_conv_kernel, L=L, da=da, db=db),
        grid=(B,),
        in_specs=[pl.BlockSpec((1, L, da), bat), pl.BlockSpec((1, L, 3 * db), bat),
                  pl.BlockSpec(cw.shape, fix), pl.BlockSpec((1, da), fix), pl.BlockSpec((1, da), fix),
                  pl.BlockSpec((1, da), fix), pl.BlockSpec(sw.shape, fix), pl.BlockSpec((1, 3 * db), fix)],
        out_specs=[pl.BlockSpec((1, L, da), bat), pl.BlockSpec((1, L, db), bat), pl.BlockSpec((1, L, db), bat)],
        out_shape=[jax.ShapeDtypeStruct((B, L, da), BF16), jax.ShapeDtypeStruct((B, L, db), BF16),
                   jax.ShapeDtypeStruct((B, L, db), BF16)],
        scratch_shapes=[pltpu.VMEM((L + 2 * CONV_PAD, da), F32), pltpu.VMEM((L + 2 * CONV_PAD, 3 * db), F32),
                        pltpu.VMEM((8, CONV_ROWS + 2 * CONV_PAD - 8, 128), F32), pltpu.VMEM((CONV_ROWS, da), F32)],
        compiler_params=_cparams(("parallel",)),
    )(a, hy, cw, cb, lg, lb, sw, sb)


def _lconv_kernel(v_ref, x0_ref, wfr_ref, wfi_ref, cw_ref, sw_ref, kare_ref, kaim_ref, kbre_ref, kbim_ref,
                  wir_ref, wii_ref, skip_ref, u_ref, xe_ref, xo_ref, ye_ref, yo_ref, *, nfb, fb):
    j = pl.program_id(1)

    @pl.when(j == 0)
    def _():
        xe_ref[...], xo_ref[...] = _split_rows(v_ref[0])

    row0 = _first_row(fb, j)
    cw = cw_ref[...]
    sw = sw_ref[...]
    are, aim, bre, bim = _half_spectra(xe_ref[...], xo_ref[...], wfr_ref[...], wfi_ref[...], cw, sw, row0)
    kare, kaim, kbre, kbim = kare_ref[...], kaim_ref[...], kbre_ref[...], kbim_ref[...]
    a, b, c, d = are * kare, aim * kaim, bre * kbre, bim * kbim
    pare = a - jnp.where(row0, 0.0, b)
    paim = jnp.where(row0, b - d, are * kaim + aim * kare)
    pbre = c - jnp.where(row0, 0.0, d)
    pbim = jnp.where(row0, aim * kbim + bim * kaim, bre * kbim + bim * kbre)
    q0re = pare + pbre
    q0im = jnp.where(row0, 2.0 * paim, paim - pbim)
    dre = pare - pbre
    dim = paim + pbim
    q1re = dre * cw - dim * sw
    q1im = jnp.where(row0, -2.0 * pbim, dre * sw + dim * cw)
    wir = wir_ref[...]
    wii = wii_ref[...]
    ce = _dot(wir, q0re.astype(BF16)) + _dot(wii, q0im.astype(BF16))
    co = _dot(wir, q1re.astype(BF16)) + _dot(wii, q1im.astype(BF16))

    @pl.when(j == 0)
    def _():
        ye_ref[...] = ce
        yo_ref[...] = co

    @pl.when(j > 0)
    def _():
        ye_ref[...] += ce
        yo_ref[...] += co

    @pl.when(j == nfb - 1)
    def _():
        skip = skip_ref[...]
        x0e, x0o = _split_rows(x0_ref[0])
        oe = x0e.astype(F32) * (0.5 * ye_ref[...] + skip * xe_ref[...].astype(F32))
        oo = x0o.astype(F32) * (0.5 * yo_ref[...] + skip * xo_ref[...].astype(F32))
        u_ref[0] = _merge_rows(oe, oo)


def _long_conv(vv, x0, wf, wi, cw, sw, kspec, skip):
    B, L, db = vv.shape
    L2 = L // 2
    fb = min(LCONV_BINS, L2)
    nfb = L2 // fb
    bat = lambda b, j: (b, 0, 0)
    blk = pl.BlockSpec((fb, db), lambda b, j: (j, 0))
    return pl.pallas_call(
        functools.partial(_lconv_kernel, nfb=nfb, fb=fb),
        grid=(B, nfb),
        in_specs=[pl.BlockSpec((1, L, db), bat), pl.BlockSpec((1, L, db), bat),
                  pl.BlockSpec((fb, L2), lambda b, j: (j, 0)), pl.BlockSpec((fb, L2), lambda b, j: (nfb + j, 0)),
                  blk, blk, blk, blk, blk, blk,
                  pl.BlockSpec((L2, fb), lambda b, j: (0, j)), pl.BlockSpec((L2, fb), lambda b, j: (0, nfb + j)),
                  pl.BlockSpec((1, db), lambda b, j: (0, 0))],
        out_specs=pl.BlockSpec((1, L, db), bat),
        out_shape=jax.ShapeDtypeStruct((B, L, db), BF16),
        scratch_shapes=[pltpu.VMEM((L2, db), BF16), pltpu.VMEM((L2, db), BF16),
                        pltpu.VMEM((L2, db), F32), pltpu.VMEM((L2, db), F32)],
        compiler_params=_cparams(("parallel", "arbitrary")),
    )(vv, x0, wf, wf, cw, sw, *kspec, wi, wi, skip)


def _post_kernel(x_ref, act_ref, u_ref, gate_ref, wa_ref, ba_ref, wb_ref, bb_ref, wo_ref, n2_ref, rw_ref,
                 x1_ref, tok_ref, aff_ref, *, E):
    tm, D = x_ref.shape
    ya = _dot(act_ref[...], wa_ref[...]) + ba_ref[...]
    yb = _dot(u_ref[...], wb_ref[...]) + bb_ref[...]
    g = gate_ref[...].astype(F32)
    m = g[:, :D] * ya + g[:, D:] * yb
    x1 = x_ref[...] + _dot(m.astype(BF16), wo_ref[...])
    x1_ref[...] = x1
    h2 = _rms(x1, n2_ref[...]).astype(BF16)
    tok_ref[:, :D // 2] = _pack_rows(h2)
    lane = lax.broadcasted_iota(I32, (tm, SIDE_LANES), 1)
    is_expert = lane < E
    logits = jnp.where(is_expert, _dot(h2, rw_ref[...]), -1e30)
    ex = jnp.exp(logits - jnp.max(logits, axis=1, keepdims=True))
    aff = jnp.where(is_expert, ex / jnp.sum(ex, axis=1, keepdims=True), 0.0)
    tok_id = (pl.program_id(0) * tm) % COMBINE_TILE + lax.broadcasted_iota(I32, (tm, SIDE_LANES), 0)
    tok_ref[:, D // 2:] = jnp.where(lane == E, tok_id, lax.bitcast_convert_type(aff, I32))
    aff_ref[...] = aff.T[0:E, :]


def _post(x, act, u, gate, wa, ba, wb, bb, wo, n2, rw, E):
    T, D = x.shape
    tm = min(TOKEN_TILE, T)
    assert COMBINE_TILE % tm == 0
    RW = D // 2 + SIDE_LANES
    row = lambda i: (i, 0)
    fix = lambda i: (0, 0)
    full = lambda a: pl.BlockSpec(a.shape, fix)
    return pl.pallas_call(
        functools.partial(_post_kernel, E=E),
        grid=(T // tm,),
        in_specs=[pl.BlockSpec((tm, D), row), pl.BlockSpec((tm, act.shape[1]), row),
                  pl.BlockSpec((tm, u.shape[1]), row), pl.BlockSpec((tm, gate.shape[1]), row),
                  full(wa), full(ba), full(wb), full(bb), full(wo), full(n2), full(rw)],
        out_specs=[pl.BlockSpec((tm, D), row), pl.BlockSpec((tm, RW), row), pl.BlockSpec((E, tm), lambda i: (0, i))],
        out_shape=[jax.ShapeDtypeStruct((T, D), F32), jax.ShapeDtypeStruct((T, RW), I32),
                   jax.ShapeDtypeStruct((E, T), F32)],
        compiler_params=_cparams(("parallel",)),
    )(x, act, u, gate, wa, ba, wb, bb, wo, n2, rw)


def _select_kernel(a_ref, idx_ref, ntab_ref, otab_ref, thr_ref, *, cap):
    E, nt, W = a_ref.shape
    nsb = cap // W
    bits = lax.bitcast_convert_type(a_ref[...], I32)

    def bisect(b, cur):
        cand = cur | jnp.left_shift(jnp.int32(1), 30 - b)
        hit = jnp.where(bits >= cand, 1.0, 0.0)
        cnt = jnp.sum(jnp.sum(hit, axis=2, keepdims=True), axis=1, keepdims=True)
        return jnp.where(cnt >= cap, cand, cur)

    thr_ref[...] = lax.fori_loop(0, 31, bisect, jnp.zeros((E, 1, 1), I32))

    r_i = lax.broadcasted_iota(I32, (W, W), 0)
    c_i = lax.broadcasted_iota(I32, (W, W), 1)
    upper = jnp.where(r_i <= c_i, 1.0, 0.0).astype(BF16)
    ones = jnp.ones((W, W), BF16)
    tr = lax.broadcasted_iota(I32, (nt, nt), 0)
    tc = lax.broadcasted_iota(I32, (nt, nt), 1)
    lower = jnp.where(tc < tr, 1.0, 0.0).astype(BF16)
    tile_id = lax.broadcasted_iota(I32, (nt, W), 0).astype(F32)
    lane = lax.broadcasted_iota(I32, (1, W), 1).astype(F32)

    def prefix(maskf):
        mb = maskf.astype(BF16)
        incl = _dot(mb, upper)
        tot = _dot(mb, ones)
        off = _dot(lower, tot.astype(BF16))
        return incl, tot, off

    def expert(e, carry):
        ae = a_ref[e]
        be = lax.bitcast_convert_type(ae, I32)
        th = thr_ref[e]
        gt = be > th
        eq = be == th
        gtf = jnp.where(gt, 1.0, 0.0)
        eqf = jnp.where(eq, 1.0, 0.0)
        need = cap - jnp.sum(jnp.sum(gtf, axis=1, keepdims=True), axis=0, keepdims=True)
        e_incl, _, e_off = prefix(eqf)
        eq_rank = e_off + e_incl - eqf
        sel = gt | (eq & (eq_rank < need))
        self_ = jnp.where(sel, 1.0, 0.0)
        incl, tot, off = prefix(self_)
        ntab_ref[e] = tot[:, 0:128]
        otab_ref[e] = off[:, 0:128]
        incl_t = incl.T.astype(BF16)
        off_hi = off + tot

        def slots(c, carry2):
            s_row = lane + jnp.asarray(c * W, F32)
            oh = jnp.where((off <= s_row) & (s_row < off_hi), 1.0, 0.0)
            tile = jnp.sum(oh * tile_id, axis=0, keepdims=True)
            rank = s_row - jnp.sum(oh * off, axis=0, keepdims=True)
            rt = _dot(incl_t, oh.astype(BF16))
            pos = jnp.sum(jnp.where(rt <= rank, 1.0, 0.0), axis=0, keepdims=True)
            idx_ref[e, pl.ds(c, 1), :] = (tile * W + pos).astype(I32)
            return carry2

        lax.fori_loop(0, nsb, slots, 0)
        return carry

    lax.fori_loop(0, E, expert, 0)


def _select(aff3, cap):
    E, nt, W = aff3.shape
    return pl.pallas_call(
        functools.partial(_select_kernel, cap=cap),
        out_shape=[jax.ShapeDtypeStruct((E, cap // W, W), I32), jax.ShapeDtypeStruct((E, nt, 128), F32),
                   jax.ShapeDtypeStruct((E, nt, 128), F32)],
        scratch_shapes=[pltpu.VMEM((E, 1, 1), I32)],
        compiler_params=pltpu.CompilerParams(vmem_limit_bytes=VMEM_LIMIT),
    )(aff3)


SC_GATHER_ROWS = 64


def _sc_gather(table, idx):
    M = idx.shape[0]
    Dw = table.shape[1]
    win = SC_GATHER_ROWS
    info = plsc.get_sparse_core_info()
    nc = info.num_cores
    nw = nc * info.num_subcores
    per_w = M // nw
    assert M % (nw * win) == 0
    mesh = plsc.VectorSubcoreMesh(core_axis_name="c", subcore_axis_name="s")

    @functools.partial(
        pl.kernel, mesh=mesh,
        out_type=jax.ShapeDtypeStruct((M, Dw), table.dtype),
        scratch_types=[pltpu.VMEM((win,), I32), pltpu.VMEM((win, Dw), table.dtype), pltpu.SemaphoreType.DMA],
    )
    def gather(table_hbm, idx_hbm, out_hbm, idx_v, rows_v, sem):
        wid = lax.axis_index("s") * nc + lax.axis_index("c")

        @pl.loop(0, per_w // win)
        def _(j):
            base = pl.multiple_of(wid * per_w + j * win, win)
            pltpu.sync_copy(idx_hbm.at[pl.ds(base, win)], idx_v)
            pltpu.async_copy(table_hbm.at[idx_v], rows_v, sem).wait()
            pltpu.sync_copy(rows_v, out_hbm.at[pl.ds(base, win)])

    return gather(table, idx)


def _ffn_kernel(xs_ref, wg_ref, wu_ref, wd_ref, ye_ref):
    R = xs_ref.shape[0]
    half = wg_ref.shape[1] // 2
    xs = _unpack_rows(xs_ref[:, :half])
    side = xs_ref[:, half:]
    lane = lax.broadcasted_iota(I32, (R, SIDE_LANES), 1)
    aff = lax.bitcast_convert_type(side, F32)
    gate = jnp.sum(jnp.where(lane == pl.program_id(0), aff, 0.0), axis=1, keepdims=True)
    g = _dot(xs, wg_ref[0])
    u = _dot(xs, wu_ref[0])
    hid = (g * jax.nn.sigmoid(g) * u).astype(BF16)
    ye = _dot(hid, wd_ref[0]) * gate
    ye_ref[:, :half] = _pack_rows(ye.astype(BF16))
    ye_ref[:, half:] = side


def _expert_ffn(xs, wg, wu, wd):
    E, D, F = wg.shape
    cap = xs.shape[0] // E
    R = min(FFN_ROWS, cap)
    assert cap % R == 0
    RW = xs.shape[1]
    nsb = cap // R
    return pl.pallas_call(
        _ffn_kernel,
        grid=(E, nsb),
        in_specs=[pl.BlockSpec((R, RW), lambda e, s: (e * nsb + s, 0)),
                  pl.BlockSpec((1, D, F), lambda e, s: (e, 0, 0)), pl.BlockSpec((1, D, F), lambda e, s: (e, 0, 0)),
                  pl.BlockSpec((1, F, D), lambda e, s: (e, 0, 0))],
        out_specs=pl.BlockSpec((R, RW), lambda e, s: (e * nsb + s, 0)),
        out_shape=jax.ShapeDtypeStruct(xs.shape, I32),
        compiler_params=_cparams(("parallel", "parallel")),
    )(xs, wg, wu, wd)


def _combine_tables(ntab, otab, cap):
    E, nt = ntab.shape
    W = COMBINE_ROWS
    g = COMBINE_TILE // ROUTE_TILE
    ntc = nt // g
    n2 = ntab.reshape(E, ntc, g).sum(-1)
    off2 = otab.reshape(E, ntc, g)[:, :, 0]
    hi = jnp.cumsum(n2, axis=0)
    lo = hi - n2
    m = hi[-1]
    nch = jnp.maximum(1, (m + W - 1) // W)
    cb_hi = jnp.cumsum(nch)
    cb_lo = cb_hi - nch
    nc_max = (E * cap) // W + ntc
    c = jnp.arange(nc_max, dtype=I32)
    tile = jnp.minimum(jnp.sum((cb_hi[None, :] <= c[:, None]).astype(I32), axis=1), ntc - 1)
    k = jnp.where(c < cb_hi[-1], c - cb_lo[tile], nch[ntc - 1] + c - cb_hi[-1])
    delta = jnp.arange(E, dtype=I32)[:, None] * cap + off2 - lo
    bcast = lambda a: jnp.broadcast_to(a.T.astype(F32)[:, :, None], (ntc, E, 128))
    return tile, k.astype(I32), m.astype(I32), nch.astype(I32), bcast(lo), bcast(hi), bcast(delta)


def _plan_kernel(tile_s, k_s, lo_ref, hi_ref, dl_ref, src_ref, *, n_rows):
    W = src_ref.shape[1]
    lane = lax.broadcasted_iota(I32, (1, W), 1).astype(F32)

    def chunk(c, carry):
        t = tile_s[c]
        q = lane + jnp.asarray(k_s[c] * W, F32)
        lo = lo_ref[t][:, 0:1]
        hi = hi_ref[t][:, 0:1]
        mine = (lo <= q) & (q < hi)
        src = jnp.sum(jnp.where(mine, dl_ref[t][:, 0:1] + q, 0.0), axis=0, keepdims=True)
        used = jnp.sum(jnp.where(mine, 1.0, 0.0), axis=0, keepdims=True) > 0.0
        p = lane + jnp.asarray(c * W, F32)
        spare = jnp.where(p < n_rows, p, p - n_rows)
        src_ref[pl.ds(c, 1), :] = jnp.where(used, src, spare).astype(I32)
        return carry

    lax.fori_loop(0, src_ref.shape[0], chunk, 0)


def _combine_plan(tile, k, lo, hi, delta, n_rows):
    smem = pl.BlockSpec(memory_space=pltpu.SMEM)
    vmem = pl.BlockSpec(memory_space=pltpu.VMEM)
    assert tile.shape[0] * COMBINE_ROWS <= 2 * n_rows
    return pl.pallas_call(
        functools.partial(_plan_kernel, n_rows=n_rows),
        in_specs=[smem, smem, vmem, vmem, vmem],
        out_specs=vmem,
        out_shape=jax.ShapeDtypeStruct((tile.shape[0], COMBINE_ROWS), I32),
        compiler_params=pltpu.CompilerParams(vmem_limit_bytes=VMEM_LIMIT),
    )(tile, k, lo, hi, delta)


def _combine_kernel(tile_s, k_s, m_s, nch_s, x1_ref, rows_ref, g_ref, y_ref, acc_ref, *, E):
    c = pl.program_id(0)
    t = tile_s[c]
    k = k_s[c]
    W = rows_ref.shape[0]
    CT, D = x1_ref.shape

    @pl.when(k == 0)
    def _():
        acc_ref[...] = x1_ref[...]

    rows = _unpack_rows(rows_ref[:, :D // 2])
    side_t = rows_ref[:, D // 2:].astype(F32).T
    tok = side_t[E:E + 1, :]
    q = k * W + lax.broadcasted_iota(I32, (1, W), 1)
    hit = (tok == lax.broadcasted_iota(I32, (CT, W), 0).astype(F32)) & (q < m_s[t])
    onehot = jnp.where(hit, 1.0, 0.0).astype(BF16)
    acc_ref[...] += _dot(onehot, rows)

    @pl.when(k == nch_s[t] - 1)
    def _():
        y_ref[...] = _rms(acc_ref[...], g_ref[...])


def _combine(tile, k, m, nch, x1, rows, g, E):
    T, D = x1.shape
    W = COMBINE_ROWS
    CT = COMBINE_TILE
    RW = rows.shape[1]
    grid_spec = pltpu.PrefetchScalarGridSpec(
        num_scalar_prefetch=4,
        grid=(tile.shape[0],),
        in_specs=[pl.BlockSpec((CT, D), lambda c, tile_s, *_: (tile_s[c], 0)),
                  pl.BlockSpec((W, RW), lambda c, *_: (c, 0)),
                  pl.BlockSpec((1, D), lambda c, *_: (0, 0))],
        out_specs=pl.BlockSpec((CT, D), lambda c, tile_s, *_: (tile_s[c], 0)),
        scratch_shapes=[pltpu.VMEM((CT, D), F32)],
    )
    return pl.pallas_call(
        functools.partial(_combine_kernel, E=E),
        grid_spec=grid_spec,
        out_shape=jax.ShapeDtypeStruct((T, D), F32),
        compiler_params=_cparams(("arbitrary",)),
    )(tile, k, m, nch, x1, rows, g)


def _layer(x, p, spec_cache):
    B, L, D = x.shape
    T = B * L
    da = p["conv_dw_w"].shape[1]
    db = p["hy_skip"].shape[1]
    E = p["n_experts"]
    W = ROUTE_TILE
    cap = max(1, min(T, CAPACITY_FACTOR * T // E))
    assert T % COMBINE_TILE == 0 and cap % W == 0 and L % CONV_ROWS == 0 and E < SIDE_LANES

    if L not in spec_cache:
        wf, wi = _dft_tables(L // 2)
        cw, sw = _twiddles(L, db)
        kspec = _filter_spectrum(L, p["hy_f_w1"], p["hy_f_b1"], p["hy_f_w2"], p["hy_f_b2"], p["hy_f_w3"],
                                 p["hy_f_b3"], p["hy_f_w4"], p["hy_f_freq"], wf, cw, sw)
        spec_cache[L] = (wf, wi, cw, sw, kspec)
    wf, wi, cw, sw, kspec = spec_cache[L]

    x2 = x.reshape(T, D)
    a, hy, gate = _inproj(x2, p["norm1_g"], p["w_in"], p["b_in"], da, db)
    act, x0, vv = _convs(a.reshape(B, L, da), hy.reshape(B, L, 3 * db), p["conv_dw_w"], p["conv_dw_b"],
                         p["conv_ln_g"], p["conv_ln_b"], p["hy_short_w"], p["hy_short_b"])
    u = _long_conv(vv, x0, wf, wi, cw, sw, kspec, p["hy_skip"])
    x1, tok, aff = _post(x2, act.reshape(T, da), u.reshape(T, db), gate, p["conv_w_out"], p["conv_b_out"],
                         p["hy_w_out"], p["hy_b_out"], p["w_o"], p["norm2_g"], p["router_w"], E)
    nt = T // W
    idx, ntab, otab = _select(aff.reshape(E, nt, W), cap)
    xs = _sc_gather(tok, idx.reshape(E * cap))
    ye = _expert_ffn(xs, p["exp_w_gate"], p["exp_w_up"], p["exp_w_down"])
    tile, k, m, nch, lo, hi, delta = _combine_tables(ntab[:, :, 0].astype(I32), otab[:, :, 0].astype(I32), cap)
    src = _combine_plan(tile, k, lo, hi, delta, E * cap)
    rows = _sc_gather(ye, src.reshape(-1))
    y = _combine(tile, k, m, nch, x1, rows, p["norm_f_g"], E)
    return y.reshape(B, L, D)


def kernel(x_prompt, x_sample, norm1_g, w_in, b_in, conv_dw_w, conv_dw_b, conv_ln_g, conv_ln_b, conv_w_out, conv_b_out, hy_short_w, hy_short_b, hy_f_w1, hy_f_b1, hy_f_w2, hy_f_b2, hy_f_w3, hy_f_b3, hy_f_w4, hy_f_freq, hy_skip, hy_w_out, hy_b_out, w_o, norm2_g, router_w, exp_w_gate, exp_w_up, exp_w_down, norm_f_g):
    depth = w_in.shape[0]
    assert depth == 1
    i = 0
    row = lambda a: a[i][None].astype(F32)
    p = {
        "norm1_g": row(norm1_g), "w_in": w_in[i].astype(BF16), "b_in": row(b_in),
        "conv_dw_w": conv_dw_w[i].astype(F32), "conv_dw_b": row(conv_dw_b),
        "conv_ln_g": row(conv_ln_g), "conv_ln_b": row(conv_ln_b),
        "conv_w_out": conv_w_out[i].astype(BF16), "conv_b_out": row(conv_b_out),
        "hy_short_w": hy_short_w[i].astype(F32), "hy_short_b": row(hy_short_b),
        "hy_f_w1": hy_f_w1[i], "hy_f_b1": hy_f_b1[i], "hy_f_w2": hy_f_w2[i], "hy_f_b2": hy_f_b2[i],
        "hy_f_w3": hy_f_w3[i], "hy_f_b3": hy_f_b3[i], "hy_f_w4": hy_f_w4[i], "hy_f_freq": hy_f_freq[i],
        "hy_skip": row(hy_skip), "hy_w_out": hy_w_out[i].astype(BF16), "hy_b_out": row(hy_b_out),
        "w_o": w_o[i].astype(BF16), "norm2_g": row(norm2_g), "n_experts": router_w.shape[2],
        "router_w": jnp.pad(router_w[i], ((0, 0), (0, SIDE_LANES - router_w.shape[2]))).astype(BF16),
        "exp_w_gate": exp_w_gate[i].astype(BF16), "exp_w_up": exp_w_up[i].astype(BF16),
        "exp_w_down": exp_w_down[i].astype(BF16), "norm_f_g": norm_f_g[None].astype(F32),
    }
    spec_cache = {}
    y_sample = _layer(x_sample, p, spec_cache)
    y_prompt = _layer(x_prompt, p, spec_cache)
    return (y_prompt, y_sample)
```

```python
import functools
import math

import jax
import jax.numpy as jnp
from jax import lax
from jax.experimental import pallas as pl
from jax.experimental.pallas import tpu as pltpu
from jax.experimental.pallas import tpu_sc as plsc

F32 = jnp.float32
BF16 = jnp.bfloat16
I32 = jnp.int32

NORM_EPS = 1e-6
LN_EPS = 1e-5
FAST_DECAY_PCT = 0.3
SLOW_DECAY_PCT = 1.5
DECAY_TARGET = 1e-2
CAPACITY_FACTOR = 2

ROUTE_TILE = 256
TOKEN_TILE = 512
CONV_ROWS = 128
CONV_PAD = 16
POST_ROWS = 256
LCONV_BINS = 512
LCONV_LANES = 256
FFN_ROWS = 1024
COMBINE_TILE = 512
COMBINE_ROWS = 512
SIDE_LANES = 128
VMEM_LIMIT = 56 * 1024 * 1024


def _cparams(sem):
    return pltpu.CompilerParams(dimension_semantics=sem, vmem_limit_bytes=VMEM_LIMIT)


def _dot(a, b):
    return jnp.dot(a, b, preferred_element_type=F32)


def _dot_nt(a, b, precision=None):
    return lax.dot_general(a, b, (((1,), (1,)), ((), ())), preferred_element_type=F32, precision=precision)


def _rms(x, g):
    return x * lax.rsqrt(jnp.mean(x * x, axis=-1, keepdims=True) + NORM_EPS) * g


_HI_MASK = -65536


def _pack_rows(xb):
    half = xb.shape[1] // 2
    lo = lax.bitcast_convert_type(xb[:, :half].astype(F32), I32)
    hi = lax.bitcast_convert_type(xb[:, half:].astype(F32), I32)
    return (hi & _HI_MASK) | lax.shift_right_logical(lo, jnp.full(lo.shape, 16, I32))


def _unpack_rows(w):
    lo = lax.bitcast_convert_type(lax.shift_left(w, jnp.full(w.shape, 16, I32)), F32)
    hi = lax.bitcast_convert_type(w & _HI_MASK, F32)
    return jnp.concatenate([lo, hi], axis=1).astype(BF16)


def _filter_kernel(z_ref, w1_ref, b1_ref, w2_ref, b2_ref, w3_ref, b3_ref, w4_ref, fr_ref, dl_ref, sd_ref):
    hp = lax.Precision.HIGHEST
    fr = fr_ref[...]
    z = z_ref[...]
    h = jnp.sin(fr * (jnp.dot(z, w1_ref[...], precision=hp, preferred_element_type=F32) + b1_ref[...]))
    h = jnp.sin(fr * (jnp.dot(h, w2_ref[...], precision=hp, preferred_element_type=F32) + b2_ref[...]))
    h = jnp.sin(fr * (jnp.dot(h, w3_ref[...], precision=hp, preferred_element_type=F32) + b3_ref[...]))
    h4 = jnp.dot(h, w4_ref[...], precision=hp, preferred_element_type=F32)
    L = z.shape[0]
    db = dl_ref.shape[1]
    t = z[:, 0:1]
    decay = jnp.exp(-t * dl_ref[...])
    hf = h4[:, :db] * decay
    hb = h4[:, db:] * decay
    norm = jnp.sum(jnp.abs(hf), axis=0, keepdims=True) + jnp.sum(jnp.abs(hb), axis=0, keepdims=True)
    hf = hf / norm
    hb = hb / norm
    row = lax.broadcasted_iota(I32, (L, 1), 0)
    hbs = jnp.where(row == 0, 0.0, pltpu.roll(hb, 1, axis=0))
    s = hf + hbs
    d = hf - hbs
    s_hi = s.astype(BF16)
    d_hi = d.astype(BF16)
    sd_ref[:, 0 * db:1 * db] = s_hi
    sd_ref[:, 1 * db:2 * db] = (s - s_hi.astype(F32)).astype(BF16)
    sd_ref[:, 2 * db:3 * db] = d_hi
    sd_ref[:, 3 * db:4 * db] = (d - d_hi.astype(F32)).astype(BF16)


def _split_rows(x):
    w = pltpu.bitcast(x, I32)
    even = lax.bitcast_convert_type(lax.shift_left(w, jnp.full(w.shape, 16, I32)), F32).astype(BF16)
    odd = lax.bitcast_convert_type(w & _HI_MASK, F32).astype(BF16)
    return even, odd


def _merge_rows(even, odd):
    e = lax.bitcast_convert_type(even.astype(BF16).astype(F32), I32)
    o = lax.bitcast_convert_type(odd.astype(BF16).astype(F32), I32)
    return pltpu.bitcast((o & _HI_MASK) | lax.shift_right_logical(e, jnp.full(e.shape, 16, I32)), BF16)


def _half_spectra(xe, xo, wfr, wfi, cw, sw, row0):
    ere = _dot(wfr, xe)
    eim = _dot(wfi, xe)
    ore = _dot(wfr, xo)
    oim = _dot(wfi, xo)
    tre = cw * ore + sw * oim
    tim = cw * oim - sw * ore
    return ere + tre, jnp.where(row0, eim, eim + tim), ere - tre, jnp.where(row0, -oim, tim - eim)


def _first_row(fb, block):
    return (lax.broadcasted_iota(I32, (fb, 1), 0) + block * fb) == 0


def _spec_kernel(sd_ref, wfr_ref, wfi_ref, cw_ref, sw_ref, kare_ref, kaim_ref, kbre_ref, kbim_ref):
    fb, db = kare_ref.shape
    row0 = _first_row(fb, pl.program_id(0))
    xe, xo = _split_rows(sd_ref[...])

    def spectra(g):
        cols = slice(g * db, (g + 1) * db)
        return _half_spectra(xe[:, cols], xo[:, cols], wfr_ref[...], wfi_ref[...], cw_ref[...], sw_ref[...], row0)

    s_hi, s_lo, d_hi, d_lo = spectra(0), spectra(1), spectra(2), spectra(3)
    kare_ref[...] = s_hi[0] + s_lo[0]
    kbre_ref[...] = s_hi[2] + s_lo[2]
    kaim_ref[...] = jnp.where(row0, s_hi[1] + s_lo[1], d_hi[1] + d_lo[1])
    kbim_ref[...] = d_hi[3] + d_lo[3]


def _dft_tables(L):
    n2 = 2 * L
    k = jnp.arange(L, dtype=I32)[:, None]
    n = jnp.arange(L, dtype=I32)[None, :]
    ph = ((k * n) % n2).astype(F32) * (2.0 * math.pi / n2)
    c = jnp.cos(ph)
    s = jnp.sin(ph)
    alt = jnp.where(n % 2 == 0, 1.0, -1.0).astype(F32)
    wf_im = jnp.where(k == 0, alt, -s)
    wf = jnp.concatenate([c, wf_im], axis=0).astype(BF16)
    ck = jnp.where(k == 0, 1.0, 2.0) / n2
    wi_re = (c * ck).T
    wi_im = jnp.where(k == 0, alt / n2, -s * ck).T
    wi = jnp.concatenate([wi_re, wi_im], axis=1).astype(BF16)
    return wf, wi


def _twiddles(L, db):
    ph = jnp.arange(L // 2, dtype=F32)[:, None] * (math.pi / L)
    return jnp.broadcast_to(jnp.cos(ph), (L // 2, db)), jnp.broadcast_to(jnp.sin(ph), (L // 2, db))


def _filter_spectrum(L, w1, b1, w2, b2, w3, b3, w4, freq, wf, cw, sw):
    pe, ffn = w1.shape
    db = w4.shape[1] // 2
    bands = (pe - 1) // 2
    t = jnp.linspace(0.0, 1.0, L, dtype=F32)[:, None]
    w = 2.0 * math.pi * jnp.arange(L, dtype=F32)[:, None] / L
    bnd = jnp.linspace(1e-4, bands - 1, bands, dtype=F32)[None, :]
    z = jnp.concatenate([t, jnp.cos(bnd * w), -jnp.sin(bnd * w)], axis=-1)
    P = 128
    zp = jnp.zeros((L, P), F32).at[:, :pe].set(z)
    pad2 = lambda a, r, c: jnp.zeros((r, c), F32).at[:a.shape[0], :a.shape[1]].set(a.astype(F32))
    max_decay = math.log(DECAY_TARGET) / FAST_DECAY_PCT
    min_decay = math.log(DECAY_TARGET) / SLOW_DECAY_PCT
    deltas = jnp.abs(jnp.linspace(min_decay, max_decay, db, dtype=F32))[None, :]
    sd = pl.pallas_call(
        _filter_kernel,
        out_shape=jax.ShapeDtypeStruct((L, 4 * db), BF16),
        compiler_params=pltpu.CompilerParams(vmem_limit_bytes=VMEM_LIMIT),
    )(zp, pad2(w1, P, P), pad2(b1[None], 1, P), pad2(w2, P, P), pad2(b2[None], 1, P), pad2(w3, P, P),
      pad2(b3[None], 1, P), pad2(w4, P, 2 * db), pad2(freq[None], 1, P), deltas)
    L2 = L // 2
    fb = min(LCONV_BINS, L2)
    nfb = L2 // fb
    blk = pl.BlockSpec((fb, db), lambda j: (j, 0))
    return pl.pallas_call(
        _spec_kernel,
        grid=(nfb,),
        in_specs=[pl.BlockSpec((L, 4 * db), lambda j: (0, 0)),
                  pl.BlockSpec((fb, L2), lambda j: (j, 0)), pl.BlockSpec((fb, L2), lambda j: (nfb + j, 0)), blk, blk],
        out_specs=[blk] * 4,
        out_shape=[jax.ShapeDtypeStruct((L2, db), F32)] * 4,
        compiler_params=_cparams(("parallel",)),
    )(sd, wf, wf, cw, sw)


def _inproj_kernel(x_ref, g_ref, w_ref, b_ref, a_ref, hy_ref, gate_ref, *, da, db):
    h = _rms(x_ref[...], g_ref[...]).astype(BF16)
    c0, c1, c2 = da, 2 * da, 2 * da + 3 * db
    za = _dot(h, w_ref[:, 0:c0]) + b_ref[:, 0:c0]
    zg = _dot(h, w_ref[:, c0:c1]) + b_ref[:, c0:c1]
    a_ref[...] = (za * jax.nn.sigmoid(zg)).astype(BF16)
    hy_ref[...] = (_dot(h, w_ref[:, c1:c2]) + b_ref[:, c1:c2]).astype(BF16)
    gate_ref[...] = jax.nn.sigmoid(_dot(h, w_ref[:, c2:]) + b_ref[:, c2:]).astype(BF16)


def _inproj(x, g, w, b, da, db):
    T, D = x.shape
    C = w.shape[1]
    tm = min(TOKEN_TILE, T)
    ng = C - 2 * da - 3 * db
    row = lambda i: (i, 0)
    fix = lambda i: (0, 0)
    return pl.pallas_call(
        functools.partial(_inproj_kernel, da=da, db=db),
        grid=(T // tm,),
        in_specs=[pl.BlockSpec((tm, D), row), pl.BlockSpec((1, D), fix), pl.BlockSpec((D, C), fix),
                  pl.BlockSpec((1, C), fix)],
        out_specs=[pl.BlockSpec((tm, da), row), pl.BlockSpec((tm, 3 * db), row), pl.BlockSpec((tm, ng), row)],
        out_shape=[jax.ShapeDtypeStruct((T, da), BF16), jax.ShapeDtypeStruct((T, 3 * db), BF16),
                   jax.ShapeDtypeStruct((T, ng), BF16)],
        compiler_params=_cparams(("parallel",)),
    )(x, g, w, b)


def _conv_kernel(a_ref, hy_ref, cw_ref, cb_ref, lg_ref, lb_ref, sw_ref, sb_ref,
                 act_ref, x0_ref, vv_ref, pa_ref, ph_ref, sh_ref, cv_ref, *, L, da, db):
    K = cw_ref.shape[0]
    KS = sw_ref.shape[0]
    R = CONV_ROWS
    P = CONV_PAD
    LANES = 128
    zeros_a = jnp.zeros((P, da), F32)
    zeros_h = jnp.zeros((P, 3 * db), F32)
    pa_ref[0:P, :] = zeros_a
    pa_ref[P + L:P + L + P, :] = zeros_a
    ph_ref[0:P, :] = zeros_h
    ph_ref[P + L:P + L + P, :] = zeros_h
    pa_ref[P:P + L, :] = a_ref[0].astype(F32)
    ph_ref[P:P + L, :] = hy_ref[0].astype(F32)

    def chunk(c, carry):
        base = pl.multiple_of(c * R, R)
        for cb in range(da // LANES):
            cols = slice(cb * LANES, (cb + 1) * LANES)
            win = pa_ref[pl.ds(base, R + 2 * P), cols]
            for r in range(8):
                sh_ref[r] = win[r:r + R + 2 * P - 8, :]
            acc = jnp.zeros((R, LANES), F32) + cb_ref[:, cols]
            for k in range(K):
                q, r = divmod(P - K // 2 + k, 8)
                acc = acc + cw_ref[k:k + 1, cols] * sh_ref[r, 8 * q:8 * q + R, :]
            cv_ref[:, cols] = acc
        acc = cv_ref[...]
        mu = jnp.mean(acc, axis=-1, keepdims=True)
        xc = acc - mu
        var = jnp.mean(xc * xc, axis=-1, keepdims=True)
        y = xc * lax.rsqrt(var + LN_EPS) * lg_ref[...] + lb_ref[...]
        act_ref[0, pl.ds(base, R), :] = (y * jax.nn.sigmoid(y)).astype(BF16)

        def short(lo):
            cols = slice(lo, lo + LANES)
            win = ph_ref[pl.ds(base, R + 2 * P), cols]
            u = jnp.zeros((R, LANES), F32) + sb_ref[:, cols]
            for k in range(KS):
                o = P - KS // 2 + k
                u = u + sw_ref[k:k + 1, cols] * win[o:o + R, :]
            return u

        for cb in range(db // LANES):
            lo = cb * LANES
            cols = slice(lo, lo + LANES)
            vv_ref[0, pl.ds(base, R), cols] = (short(db + lo) * short(2 * db + lo)).astype(BF16)
            x0_ref[0, pl.ds(base, R), cols] = short(lo).astype(BF16)
        return carry

    lax.fori_loop(0, L // R, chunk, 0)


def _convs(a, hy, cw, cb, lg, lb, sw, sb):
    B, L, da = a.shape
    db = hy.shape[2] // 3
    bat = lambda b: (b, 0, 0)
    fix = lambda b: (0, 0)
    return pl.pallas_call(
        functools.partial(_conv_kernel, L=L, da=da, db=db),
        grid=(B,),
        in_specs=[pl.BlockSpec((1, L, da), bat), pl.BlockSpec((1, L, 3 * db), bat),
                  pl.BlockSpec(cw.shape, fix), pl.BlockSpec((1, da), fix), pl.BlockSpec((1, da), fix),
                  pl.BlockSpec((1, da), fix), pl.BlockSpec(sw.shape, fix), pl.BlockSpec((1, 3 * db), fix)],
        out_specs=[pl.BlockSpec((1, L, da), bat), pl.BlockSpec((1, L, db), bat), pl.BlockSpec((1, L, db), bat)],
        out_shape=[jax.ShapeDtypeStruct((B, L, da), BF16), jax.ShapeDtypeStruct((B, L, db), BF16),
                   jax.ShapeDtypeStruct((B, L, db), BF16)],
        scratch_shapes=[pltpu.VMEM((L + 2 * CONV_PAD, da), F32), pltpu.VMEM((L + 2 * CONV_PAD, 3 * db), F32),
                        pltpu.VMEM((8, CONV_ROWS + 2 * CONV_PAD - 8, 128), F32), pltpu.VMEM((CONV_ROWS, da), F32)],
        compiler_params=_cparams(("parallel",)),
    )(a, hy, cw, cb, lg, lb, sw, sb)


def _lconv_kernel(v_ref, x0_ref, wfr_ref, wfi_ref, cw_ref, sw_ref, kare_ref, kaim_ref, kbre_ref, kbim_ref,
                  wir_ref, wii_ref, skip_ref, u_ref, xe_ref, xo_ref, ye_ref, yo_ref, *, nfb, fb):
    j = pl.program_id(1)

    @pl.when(j == 0)
    def _():
        xe_ref[...], xo_ref[...] = _split_rows(v_ref[0])
        ye_ref[...] = jnp.zeros(ye_ref.shape, F32)
        yo_ref[...] = jnp.zeros(yo_ref.shape, F32)

    row0 = _first_row(fb, j)
    wfr, wfi, wir, wii = wfr_ref[...], wfi_ref[...], wir_ref[...], wii_ref[...]
    G = min(LCONV_LANES, v_ref.shape[2])
    for g in range(v_ref.shape[2] // G):
        cols = slice(g * G, (g + 1) * G)
        cw = cw_ref[:, cols]
        sw = sw_ref[:, cols]
        are, aim, bre, bim = _half_spectra(xe_ref[:, cols], xo_ref[:, cols], wfr, wfi, cw, sw, row0)
        kare, kaim, kbre, kbim = kare_ref[:, cols], kaim_ref[:, cols], kbre_ref[:, cols], kbim_ref[:, cols]
        a, b, c, d = are * kare, aim * kaim, bre * kbre, bim * kbim
        pare = a - jnp.where(row0, 0.0, b)
        paim = jnp.where(row0, b - d, are * kaim + aim * kare)
        pbre = c - jnp.where(row0, 0.0, d)
        pbim = jnp.where(row0, aim * kbim + bim * kaim, bre * kbim + bim * kbre)
        q0re = pare + pbre
        q0im = jnp.where(row0, 2.0 * paim, paim - pbim)
        dre = pare - pbre
        dim = paim + pbim
        q1re = dre * cw - dim * sw
        q1im = jnp.where(row0, -2.0 * pbim, dre * sw + dim * cw)
        ye_ref[:, cols] += _dot(wir, q0re.astype(BF16)) + _dot(wii, q0im.astype(BF16))
        yo_ref[:, cols] += _dot(wir, q1re.astype(BF16)) + _dot(wii, q1im.astype(BF16))

    @pl.when(j == nfb - 1)
    def _():
        skip = skip_ref[...]
        x0e, x0o = _split_rows(x0_ref[0])
        oe = x0e.astype(F32) * (0.5 * ye_ref[...] + skip * xe_ref[...].astype(F32))
        oo = x0o.astype(F32) * (0.5 * yo_ref[...] + skip * xo_ref[...].astype(F32))
        u_ref[0] = _merge_rows(oe, oo)


def _long_conv(vv, x0, wf, wi, cw, sw, kspec, skip):
    B, L, db = vv.shape
    L2 = L // 2
    fb = min(LCONV_BINS, L2)
    nfb = L2 // fb
    bat = lambda b, j: (b, 0, 0)
    blk = pl.BlockSpec((fb, db), lambda b, j: (j, 0))
    return pl.pallas_call(
        functools.partial(_lconv_kernel, nfb=nfb, fb=fb),
        grid=(B, nfb),
        in_specs=[pl.BlockSpec((1, L, db), bat), pl.BlockSpec((1, L, db), bat),
                  pl.BlockSpec((fb, L2), lambda b, j: (j, 0)), pl.BlockSpec((fb, L2), lambda b, j: (nfb + j, 0)),
                  blk, blk, blk, blk, blk, blk,
                  pl.BlockSpec((L2, fb), lambda b, j: (0, j)), pl.BlockSpec((L2, fb), lambda b, j: (0, nfb + j)),
                  pl.BlockSpec((1, db), lambda b, j: (0, 0))],
        out_specs=pl.BlockSpec((1, L, db), bat),
        out_shape=jax.ShapeDtypeStruct((B, L, db), BF16),
        scratch_shapes=[pltpu.VMEM((L2, db), BF16), pltpu.VMEM((L2, db), BF16),
                        pltpu.VMEM((L2, db), F32), pltpu.VMEM((L2, db), F32)],
        compiler_params=_cparams(("parallel", "arbitrary")),
    )(vv, x0, wf, wf, cw, sw, *kspec, wi, wi, skip)


def _post_kernel(x_ref, act_ref, u_ref, gate_ref, wa_ref, ba_ref, wb_ref, bb_ref, wo_ref, n2_ref, rw_ref,
                 x1_ref, tok_ref, aff_ref, *, E):
    tm, D = x_ref.shape
    RS = min(POST_ROWS, tm)
    for r0 in range(0, tm, RS):
        rows = slice(r0, r0 + RS)
        ya = _dot(act_ref[rows, :], wa_ref[...]) + ba_ref[...]
        yb = _dot(u_ref[rows, :], wb_ref[...]) + bb_ref[...]
        m = gate_ref[rows, :D].astype(F32) * ya + gate_ref[rows, D:].astype(F32) * yb
        x1 = x_ref[rows, :] + _dot(m.astype(BF16), wo_ref[...])
        x1_ref[rows, :] = x1
        h2 = _rms(x1, n2_ref[...]).astype(BF16)
        tok_ref[rows, :D // 2] = _pack_rows(h2)
        lane = lax.broadcasted_iota(I32, (RS, SIDE_LANES), 1)
        is_expert = lane < E
        logits = jnp.where(is_expert, _dot(h2, rw_ref[...]), -1e30)
        ex = jnp.exp(logits - jnp.max(logits, axis=1, keepdims=True))
        aff = jnp.where(is_expert, ex / jnp.sum(ex, axis=1, keepdims=True), 0.0)
        tok_id = (pl.program_id(0) * tm + r0) % COMBINE_TILE + lax.broadcasted_iota(I32, (RS, SIDE_LANES), 0)
        tok_ref[rows, D // 2:] = jnp.where(lane == E, tok_id, lax.bitcast_convert_type(aff, I32))
        aff_ref[:, rows] = aff.T[0:E, :]


def _post(x, act, u, gate, wa, ba, wb, bb, wo, n2, rw, E):
    T, D = x.shape
    tm = min(TOKEN_TILE, T)
    assert COMBINE_TILE % tm == 0
    RW = D // 2 + SIDE_LANES
    row = lambda i: (i, 0)
    fix = lambda i: (0, 0)
    full = lambda a: pl.BlockSpec(a.shape, fix)
    return pl.pallas_call(
        functools.partial(_post_kernel, E=E),
        grid=(T // tm,),
        in_specs=[pl.BlockSpec((tm, D), row), pl.BlockSpec((tm, act.shape[1]), row),
                  pl.BlockSpec((tm, u.shape[1]), row), pl.BlockSpec((tm, gate.shape[1]), row),
                  full(wa), full(ba), full(wb), full(bb), full(wo), full(n2), full(rw)],
        out_specs=[pl.BlockSpec((tm, D), row), pl.BlockSpec((tm, RW), row), pl.BlockSpec((E, tm), lambda i: (0, i))],
        out_shape=[jax.ShapeDtypeStruct((T, D), F32), jax.ShapeDtypeStruct((T, RW), I32),
                   jax.ShapeDtypeStruct((E, T), F32)],
        compiler_params=_cparams(("parallel",)),
    )(x, act, u, gate, wa, ba, wb, bb, wo, n2, rw)


def _select_kernel(a_ref, idx_ref, ntab_ref, otab_ref, thr_ref, *, cap):
    E, nt, W = a_ref.shape
    nsb = cap // W
    bits = lax.bitcast_convert_type(a_ref[...], I32)

    def bisect(b, cur):
        cand = cur | jnp.left_shift(jnp.int32(1), 30 - b)
        hit = jnp.where(bits >= cand, 1.0, 0.0)
        cnt = jnp.sum(jnp.sum(hit, axis=2, keepdims=True), axis=1, keepdims=True)
        return jnp.where(cnt >= cap, cand, cur)

    thr_ref[...] = lax.fori_loop(0, 31, bisect, jnp.zeros((E, 1, 1), I32))

    r_i = lax.broadcasted_iota(I32, (W, W), 0)
    c_i = lax.broadcasted_iota(I32, (W, W), 1)
    upper = jnp.where(r_i <= c_i, 1.0, 0.0).astype(BF16)
    ones = jnp.ones((W, W), BF16)
    tr = lax.broadcasted_iota(I32, (nt, nt), 0)
    tc = lax.broadcasted_iota(I32, (nt, nt), 1)
    lower = jnp.where(tc < tr, 1.0, 0.0).astype(BF16)
    tile_id = lax.broadcasted_iota(I32, (nt, W), 0).astype(F32)
    lane = lax.broadcasted_iota(I32, (1, W), 1).astype(F32)

    def prefix(maskf):
        mb = maskf.astype(BF16)
        incl = _dot(mb, upper)
        tot = _dot(mb, ones)
        off = _dot(lower, tot.astype(BF16))
        return incl, tot, off

    def expert(e, carry):
        ae = a_ref[e]
        be = lax.bitcast_convert_type(ae, I32)
        th = thr_ref[e]
        gt = be > th
        eq = be == th
        gtf = jnp.where(gt, 1.0, 0.0)
        eqf = jnp.where(eq, 1.0, 0.0)
        need = cap - jnp.sum(jnp.sum(gtf, axis=1, keepdims=True), axis=0, keepdims=True)
        e_incl, _, e_off = prefix(eqf)
        eq_rank = e_off + e_incl - eqf
        sel = gt | (eq & (eq_rank < need))
        self_ = jnp.where(sel, 1.0, 0.0)
        incl, tot, off = prefix(self_)
        ntab_ref[e] = tot[:, 0:128]
        otab_ref[e] = off[:, 0:128]
        incl_t = incl.T.astype(BF16)
        off_hi = off + tot

        def slots(c, carry2):
            s_row = lane + jnp.asarray(c * W, F32)
            oh = jnp.where((off <= s_row) & (s_row < off_hi), 1.0, 0.0)
            tile = jnp.sum(oh * tile_id, axis=0, keepdims=True)
            rank = s_row - jnp.sum(oh * off, axis=0, keepdims=True)
            rt = _dot(incl_t, oh.astype(BF16))
            pos = jnp.sum(jnp.where(rt <= rank, 1.0, 0.0), axis=0, keepdims=True)
            idx_ref[e, pl.ds(c, 1), :] = (tile * W + pos).astype(I32)
            return carry2

        lax.fori_loop(0, nsb, slots, 0)
        return carry

    lax.fori_loop(0, E, expert, 0)


def _select(aff3, cap):
    E, nt, W = aff3.shape
    return pl.pallas_call(
        functools.partial(_select_kernel, cap=cap),
        out_shape=[jax.ShapeDtypeStruct((E, cap // W, W), I32), jax.ShapeDtypeStruct((E, nt, 128), F32),
                   jax.ShapeDtypeStruct((E, nt, 128), F32)],
        scratch_shapes=[pltpu.VMEM((E, 1, 1), I32)],
        compiler_params=pltpu.CompilerParams(vmem_limit_bytes=VMEM_LIMIT),
    )(aff3)


SC_GATHER_ROWS = 64


def _sc_gather(table, idx):
    M = idx.shape[0]
    Dw = table.shape[1]
    win = SC_GATHER_ROWS
    info = plsc.get_sparse_core_info()
    nc = info.num_cores
    nw = nc * info.num_subcores
    per_w = M // nw
    assert M % (nw * win) == 0
    mesh = plsc.VectorSubcoreMesh(core_axis_name="c", subcore_axis_name="s")

    @functools.partial(
        pl.kernel, mesh=mesh,
        out_type=jax.ShapeDtypeStruct((M, Dw), table.dtype),
        scratch_types=[pltpu.VMEM((win,), I32), pltpu.VMEM((win, Dw), table.dtype), pltpu.SemaphoreType.DMA],
    )
    def gather(table_hbm, idx_hbm, out_hbm, idx_v, rows_v, sem):
        wid = lax.axis_index("s") * nc + lax.axis_index("c")

        @pl.loop(0, per_w // win)
        def _(j):
            base = pl.multiple_of(wid * per_w + j * win, win)
            pltpu.sync_copy(idx_hbm.at[pl.ds(base, win)], idx_v)
            pltpu.async_copy(table_hbm.at[idx_v], rows_v, sem).wait()
            pltpu.sync_copy(rows_v, out_hbm.at[pl.ds(base, win)])

    return gather(table, idx)


def _ffn_kernel(xs_ref, wg_ref, wu_ref, wd_ref, ye_ref):
    R = xs_ref.shape[0]
    half = wg_ref.shape[1] // 2
    xs = _unpack_rows(xs_ref[:, :half])
    side = xs_ref[:, half:]
    lane = lax.broadcasted_iota(I32, (R, SIDE_LANES), 1)
    aff = lax.bitcast_convert_type(side, F32)
    gate = jnp.sum(jnp.where(lane == pl.program_id(0), aff, 0.0), axis=1, keepdims=True)
    g = _dot(xs, wg_ref[0])
    u = _dot(xs, wu_ref[0])
    hid = (g * jax.nn.sigmoid(g) * u).astype(BF16)
    ye = _dot(hid, wd_ref[0]) * gate
    ye_ref[:, :half] = _pack_rows(ye.astype(BF16))
    ye_ref[:, half:] = side


def _expert_ffn(xs, wg, wu, wd):
    E, D, F = wg.shape
    cap = xs.shape[0] // E
    R = min(FFN_ROWS, cap)
    assert cap % R == 0
    RW = xs.shape[1]
    nsb = cap // R
    return pl.pallas_call(
        _ffn_kernel,
        grid=(E, nsb),
        in_specs=[pl.BlockSpec((R, RW), lambda e, s: (e * nsb + s, 0)),
                  pl.BlockSpec((1, D, F), lambda e, s: (e, 0, 0)), pl.BlockSpec((1, D, F), lambda e, s: (e, 0, 0)),
                  pl.BlockSpec((1, F, D), lambda e, s: (e, 0, 0))],
        out_specs=pl.BlockSpec((R, RW), lambda e, s: (e * nsb + s, 0)),
        out_shape=jax.ShapeDtypeStruct(xs.shape, I32),
        compiler_params=_cparams(("parallel", "parallel")),
    )(xs, wg, wu, wd)


def _combine_tables(ntab, otab, cap):
    E, nt = ntab.shape
    W = COMBINE_ROWS
    g = COMBINE_TILE // ROUTE_TILE
    ntc = nt // g
    n2 = ntab.reshape(E, ntc, g).sum(-1)
    off2 = otab.reshape(E, ntc, g)[:, :, 0]
    hi = jnp.cumsum(n2, axis=0)
    lo = hi - n2
    m = hi[-1]
    nch = jnp.maximum(1, (m + W - 1) // W)
    cb_hi = jnp.cumsum(nch)
    cb_lo = cb_hi - nch
    nc_max = (E * cap) // W + ntc
    c = jnp.arange(nc_max, dtype=I32)
    tile = jnp.minimum(jnp.sum((cb_hi[None, :] <= c[:, None]).astype(I32), axis=1), ntc - 1)
    k = jnp.where(c < cb_hi[-1], c - cb_lo[tile], nch[ntc - 1] + c - cb_hi[-1])
    delta = jnp.arange(E, dtype=I32)[:, None] * cap + off2 - lo
    bcast = lambda a: jnp.broadcast_to(a.T.astype(F32)[:, :, None], (ntc, E, 128))
    blk = jnp.minimum(c, cb_hi[-1] - 1)
    sched = (tile, k.astype(I32), m.astype(I32), nch.astype(I32), blk.astype(I32))
    return sched, bcast(lo), bcast(hi), bcast(delta)


def _plan_kernel(tile_s, k_s, lo_ref, hi_ref, dl_ref, src_ref, *, n_rows):
    W = src_ref.shape[1]
    lane = lax.broadcasted_iota(I32, (1, W), 1).astype(F32)

    def chunk(c, carry):
        t = tile_s[c]
        q = lane + jnp.asarray(k_s[c] * W, F32)
        lo = lo_ref[t][:, 0:1]
        hi = hi_ref[t][:, 0:1]
        mine = (lo <= q) & (q < hi)
        src = jnp.sum(jnp.where(mine, dl_ref[t][:, 0:1] + q, 0.0), axis=0, keepdims=True)
        used = jnp.sum(jnp.where(mine, 1.0, 0.0), axis=0, keepdims=True) > 0.0
        p = lane + jnp.asarray(c * W, F32)
        spare = jnp.where(p < n_rows, p, p - n_rows)
        src_ref[pl.ds(c, 1), :] = jnp.where(used, src, spare).astype(I32)
        return carry

    lax.fori_loop(0, src_ref.shape[0], chunk, 0)


def _combine_plan(tile, k, lo, hi, delta, n_rows):
    smem = pl.BlockSpec(memory_space=pltpu.SMEM)
    vmem = pl.BlockSpec(memory_space=pltpu.VMEM)
    assert tile.shape[0] * COMBINE_ROWS <= 2 * n_rows
    return pl.pallas_call(
        functools.partial(_plan_kernel, n_rows=n_rows),
        in_specs=[smem, smem, vmem, vmem, vmem],
        out_specs=vmem,
        out_shape=jax.ShapeDtypeStruct((tile.shape[0], COMBINE_ROWS), I32),
        compiler_params=pltpu.CompilerParams(vmem_limit_bytes=VMEM_LIMIT),
    )(tile, k, lo, hi, delta)


def _combine_kernel(tile_s, k_s, m_s, nch_s, blk_s, x1_ref, rows_ref, g_ref, y_ref, acc_ref, *, E):
    c = pl.program_id(0)
    t = tile_s[c]
    k = k_s[c]
    W = rows_ref.shape[0]
    CT, D = x1_ref.shape

    @pl.when(k == 0)
    def _():
        acc_ref[...] = x1_ref[...]

    @pl.when(k < nch_s[t])
    def _():
        rows = _unpack_rows(rows_ref[:, :D // 2])
        side_t = rows_ref[:, D // 2:].astype(F32).T
        tok = side_t[E:E + 1, :]
        q = k * W + lax.broadcasted_iota(I32, (1, W), 1)
        hit = (tok == lax.broadcasted_iota(I32, (CT, W), 0).astype(F32)) & (q < m_s[t])
        onehot = jnp.where(hit, 1.0, 0.0).astype(BF16)
        acc_ref[...] += _dot(onehot, rows)

    @pl.when(k == nch_s[t] - 1)
    def _():
        y_ref[...] = _rms(acc_ref[...], g_ref[...])


def _combine(sched, x1, rows, g, E):
    T, D = x1.shape
    W = COMBINE_ROWS
    CT = COMBINE_TILE
    RW = rows.shape[1]
    grid_spec = pltpu.PrefetchScalarGridSpec(
        num_scalar_prefetch=len(sched),
        grid=(sched[0].shape[0],),
        in_specs=[pl.BlockSpec((CT, D), lambda c, tile_s, *_: (tile_s[c], 0)),
                  pl.BlockSpec((W, RW), lambda c, tile_s, k_s, m_s, nch_s, blk_s: (blk_s[c], 0)),
                  pl.BlockSpec((1, D), lambda c, *_: (0, 0))],
        out_specs=pl.BlockSpec((CT, D), lambda c, tile_s, *_: (tile_s[c], 0)),
        scratch_shapes=[pltpu.VMEM((CT, D), F32)],
    )
    return pl.pallas_call(
        functools.partial(_combine_kernel, E=E),
        grid_spec=grid_spec,
        out_shape=jax.ShapeDtypeStruct((T, D), F32),
        compiler_params=_cparams(("arbitrary",)),
    )(*sched, x1, rows, g)


def _layer(x, p, spec_cache):
    B, L, D = x.shape
    T = B * L
    da = p["conv_dw_w"].shape[1]
    db = p["hy_skip"].shape[1]
    E = p["n_experts"]
    W = ROUTE_TILE
    cap = max(1, min(T, CAPACITY_FACTOR * T // E))
    assert T % COMBINE_TILE == 0 and cap % W == 0 and L % CONV_ROWS == 0 and E < SIDE_LANES

    if L not in spec_cache:
        wf, wi = _dft_tables(L // 2)
        cw, sw = _twiddles(L, db)
        kspec = _filter_spectrum(L, p["hy_f_w1"], p["hy_f_b1"], p["hy_f_w2"], p["hy_f_b2"], p["hy_f_w3"],
                                 p["hy_f_b3"], p["hy_f_w4"], p["hy_f_freq"], wf, cw, sw)
        spec_cache[L] = (wf, wi, cw, sw, kspec)
    wf, wi, cw, sw, kspec = spec_cache[L]

    x2 = x.reshape(T, D)
    a, hy, gate = _inproj(x2, p["norm1_g"], p["w_in"], p["b_in"], da, db)
    act, x0, vv = _convs(a.reshape(B, L, da), hy.reshape(B, L, 3 * db), p["conv_dw_w"], p["conv_dw_b"],
                         p["conv_ln_g"], p["conv_ln_b"], p["hy_short_w"], p["hy_short_b"])
    u = _long_conv(vv, x0, wf, wi, cw, sw, kspec, p["hy_skip"])
    x1, tok, aff = _post(x2, act.reshape(T, da), u.reshape(T, db), gate, p["conv_w_out"], p["conv_b_out"],
                         p["hy_w_out"], p["hy_b_out"], p["w_o"], p["norm2_g"], p["router_w"], E)
    nt = T // W
    idx, ntab, otab = _select(aff.reshape(E, nt, W), cap)
    xs = _sc_gather(tok, idx.reshape(E * cap))
    ye = _expert_ffn(xs, p["exp_w_gate"], p["exp_w_up"], p["exp_w_down"])
    sched, lo, hi, delta = _combine_tables(ntab[:, :, 0].astype(I32), otab[:, :, 0].astype(I32), cap)
    src = _combine_plan(sched[0], sched[1], lo, hi, delta, E * cap)
    rows = _sc_gather(ye, src.reshape(-1))
    y = _combine(sched, x1, rows, p["norm_f_g"], E)
    return y.reshape(B, L, D)


def kernel(x_prompt, x_sample, norm1_g, w_in, b_in, conv_dw_w, conv_dw_b, conv_ln_g, conv_ln_b, conv_w_out, conv_b_out, hy_short_w, hy_short_b, hy_f_w1, hy_f_b1, hy_f_w2, hy_f_b2, hy_f_w3, hy_f_b3, hy_f_w4, hy_f_freq, hy_skip, hy_w_out, hy_b_out, w_o, norm2_g, router_w, exp_w_gate, exp_w_up, exp_w_down, norm_f_g):
    depth = w_in.shape[0]
    assert depth == 1
    i = 0
    row = lambda a: a[i][None].astype(F32)
    p = {
        "norm1_g": row(norm1_g), "w_in": w_in[i].astype(BF16), "b_in": row(b_in),
        "conv_dw_w": conv_dw_w[i].astype(F32), "conv_dw_b": row(conv_dw_b),
        "conv_ln_g": row(conv_ln_g), "conv_ln_b": row(conv_ln_b),
        "conv_w_out": conv_w_out[i].astype(BF16), "conv_b_out": row(conv_b_out),
        "hy_short_w": hy_short_w[i].astype(F32), "hy_short_b": row(hy_short_b),
        "hy_f_w1": hy_f_w1[i], "hy_f_b1": hy_f_b1[i], "hy_f_w2": hy_f_w2[i], "hy_f_b2": hy_f_b2[i],
        "hy_f_w3": hy_f_w3[i], "hy_f_b3": hy_f_b3[i], "hy_f_w4": hy_f_w4[i], "hy_f_freq": hy_f_freq[i],
        "hy_skip": row(hy_skip), "hy_w_out": hy_w_out[i].astype(BF16), "hy_b_out": row(hy_b_out),
        "w_o": w_o[i].astype(BF16), "norm2_g": row(norm2_g), "n_experts": router_w.shape[2],
        "router_w": jnp.pad(router_w[i], ((0, 0), (0, SIDE_LANES - router_w.shape[2]))).astype(BF16),
        "exp_w_gate": exp_w_gate[i].astype(BF16), "exp_w_up": exp_w_up[i].astype(BF16),
        "exp_w_down": exp_w_down[i].astype(BF16), "norm_f_g": norm_f_g[None].astype(F32),
    }
    spec_cache = {}
    y_sample = _layer(x_sample, p, spec_cache)
    y_prompt = _layer(x_prompt, p, spec_cache)
    return (y_prompt, y_sample)
```

```python
import functools
import math

import jax
import jax.numpy as jnp
from jax import lax
from jax.experimental import pallas as pl
from jax.experimental.pallas import tpu as pltpu
from jax.experimental.pallas import tpu_sc as plsc

F32 = jnp.float32
BF16 = jnp.bfloat16
I32 = jnp.int32

NORM_EPS = 1e-6
LN_EPS = 1e-5
FAST_DECAY_PCT = 0.3
SLOW_DECAY_PCT = 1.5
DECAY_TARGET = 1e-2
CAPACITY_FACTOR = 2

ROUTE_TILE = 256
TOKEN_TILE = 512
CONV_ROWS = 128
CONV_PAD = 16
POST_ROWS = 256
LCONV_BINS = 512
LCONV_LANES = 256
FFN_ROWS = 1024
COMBINE_TILE = 512
COMBINE_ROWS = 512
SIDE_LANES = 128
VMEM_LIMIT = 56 * 1024 * 1024


def _cparams(sem):
    return pltpu.CompilerParams(dimension_semantics=sem, vmem_limit_bytes=VMEM_LIMIT)


def _dot(a, b):
    return jnp.dot(a, b, preferred_element_type=F32)


def _dot_nt(a, b, precision=None):
    return lax.dot_general(a, b, (((1,), (1,)), ((), ())), preferred_element_type=F32, precision=precision)


def _rms(x, g):
    return x * lax.rsqrt(jnp.mean(x * x, axis=-1, keepdims=True) + NORM_EPS) * g


_HI_MASK = -65536


def _pack_rows(xb):
    half = xb.shape[1] // 2
    lo = lax.bitcast_convert_type(xb[:, :half].astype(F32), I32)
    hi = lax.bitcast_convert_type(xb[:, half:].astype(F32), I32)
    return (hi & _HI_MASK) | lax.shift_right_logical(lo, jnp.full(lo.shape, 16, I32))


def _unpack_rows(w):
    lo = lax.bitcast_convert_type(lax.shift_left(w, jnp.full(w.shape, 16, I32)), F32)
    hi = lax.bitcast_convert_type(w & _HI_MASK, F32)
    return jnp.concatenate([lo, hi], axis=1).astype(BF16)


def _filter_kernel(z_ref, w1_ref, b1_ref, w2_ref, b2_ref, w3_ref, b3_ref, w4_ref, fr_ref, dl_ref, sd_ref):
    hp = lax.Precision.HIGHEST
    fr = fr_ref[...]
    z = z_ref[...]
    h = jnp.sin(fr * (jnp.dot(z, w1_ref[...], precision=hp, preferred_element_type=F32) + b1_ref[...]))
    h = jnp.sin(fr * (jnp.dot(h, w2_ref[...], precision=hp, preferred_element_type=F32) + b2_ref[...]))
    h = jnp.sin(fr * (jnp.dot(h, w3_ref[...], precision=hp, preferred_element_type=F32) + b3_ref[...]))
    h4 = jnp.dot(h, w4_ref[...], precision=hp, preferred_element_type=F32)
    L = z.shape[0]
    db = dl_ref.shape[1]
    t = z[:, 0:1]
    decay = jnp.exp(-t * dl_ref[...])
    hf = h4[:, :db] * decay
    hb = h4[:, db:] * decay
    norm = jnp.sum(jnp.abs(hf), axis=0, keepdims=True) + jnp.sum(jnp.abs(hb), axis=0, keepdims=True)
    hf = hf / norm
    hb = hb / norm
    row = lax.broadcasted_iota(I32, (L, 1), 0)
    hbs = jnp.where(row == 0, 0.0, pltpu.roll(hb, 1, axis=0))
    s = hf + hbs
    d = hf - hbs
    s_hi = s.astype(BF16)
    d_hi = d.astype(BF16)
    sd_ref[:, 0 * db:1 * db] = s_hi
    sd_ref[:, 1 * db:2 * db] = (s - s_hi.astype(F32)).astype(BF16)
    sd_ref[:, 2 * db:3 * db] = d_hi
    sd_ref[:, 3 * db:4 * db] = (d - d_hi.astype(F32)).astype(BF16)


def _split_rows(x):
    w = pltpu.bitcast(x, I32)
    even = lax.bitcast_convert_type(lax.shift_left(w, jnp.full(w.shape, 16, I32)), F32).astype(BF16)
    odd = lax.bitcast_convert_type(w & _HI_MASK, F32).astype(BF16)
    return even, odd


def _merge_rows(even, odd):
    e = lax.bitcast_convert_type(even.astype(BF16).astype(F32), I32)
    o = lax.bitcast_convert_type(odd.astype(BF16).astype(F32), I32)
    return pltpu.bitcast((o & _HI_MASK) | lax.shift_right_logical(e, jnp.full(e.shape, 16, I32)), BF16)


def _half_spectra(xe, xo, wfr, wfi, cw, sw, row0):
    ere = _dot(wfr, xe)
    eim = _dot(wfi, xe)
    ore = _dot(wfr, xo)
    oim = _dot(wfi, xo)
    tre = cw * ore + sw * oim
    tim = cw * oim - sw * ore
    return ere + tre, jnp.where(row0, eim, eim + tim), ere - tre, jnp.where(row0, -oim, tim - eim)


def _first_row(fb, block):
    return (lax.broadcasted_iota(I32, (fb, 1), 0) + block * fb) == 0


def _spec_kernel(sd_ref, wfr_ref, wfi_ref, cw_ref, sw_ref, kare_ref, kaim_ref, kbre_ref, kbim_ref):
    fb, db = kare_ref.shape
    row0 = _first_row(fb, pl.program_id(0))
    xe, xo = _split_rows(sd_ref[...])

    def spectra(g):
        cols = slice(g * db, (g + 1) * db)
        return _half_spectra(xe[:, cols], xo[:, cols], wfr_ref[...], wfi_ref[...], cw_ref[...], sw_ref[...], row0)

    s_hi, s_lo, d_hi, d_lo = spectra(0), spectra(1), spectra(2), spectra(3)
    kare_ref[...] = s_hi[0] + s_lo[0]
    kbre_ref[...] = s_hi[2] + s_lo[2]
    kaim_ref[...] = jnp.where(row0, s_hi[1] + s_lo[1], d_hi[1] + d_lo[1])
    kbim_ref[...] = d_hi[3] + d_lo[3]


def _dft_tables(L):
    n2 = 2 * L
    k = jnp.arange(L, dtype=I32)[:, None]
    n = jnp.arange(L, dtype=I32)[None, :]
    ph = ((k * n) % n2).astype(F32) * (2.0 * math.pi / n2)
    c = jnp.cos(ph)
    s = jnp.sin(ph)
    alt = jnp.where(n % 2 == 0, 1.0, -1.0).astype(F32)
    wf_im = jnp.where(k == 0, alt, -s)
    wf = jnp.concatenate([c, wf_im], axis=0).astype(BF16)
    ck = jnp.where(k == 0, 1.0, 2.0) / n2
    wi_re = (c * ck).T
    wi_im = jnp.where(k == 0, alt / n2, -s * ck).T
    wi = jnp.concatenate([wi_re, wi_im], axis=1).astype(BF16)
    return wf, wi


def _twiddles(L, db):
    ph = jnp.arange(L // 2, dtype=F32)[:, None] * (math.pi / L)
    return jnp.broadcast_to(jnp.cos(ph), (L // 2, db)), jnp.broadcast_to(jnp.sin(ph), (L // 2, db))


def _filter_spectrum(L, w1, b1, w2, b2, w3, b3, w4, freq, wf, cw, sw):
    pe, ffn = w1.shape
    db = w4.shape[1] // 2
    bands = (pe - 1) // 2
    t = jnp.linspace(0.0, 1.0, L, dtype=F32)[:, None]
    w = 2.0 * math.pi * jnp.arange(L, dtype=F32)[:, None] / L
    bnd = jnp.linspace(1e-4, bands - 1, bands, dtype=F32)[None, :]
    z = jnp.concatenate([t, jnp.cos(bnd * w), -jnp.sin(bnd * w)], axis=-1)
    P = 128
    zp = jnp.zeros((L, P), F32).at[:, :pe].set(z)
    pad2 = lambda a, r, c: jnp.zeros((r, c), F32).at[:a.shape[0], :a.shape[1]].set(a.astype(F32))
    max_decay = math.log(DECAY_TARGET) / FAST_DECAY_PCT
    min_decay = math.log(DECAY_TARGET) / SLOW_DECAY_PCT
    deltas = jnp.abs(jnp.linspace(min_decay, max_decay, db, dtype=F32))[None, :]
    sd = pl.pallas_call(
        _filter_kernel,
        out_shape=jax.ShapeDtypeStruct((L, 4 * db), BF16),
        compiler_params=pltpu.CompilerParams(vmem_limit_bytes=VMEM_LIMIT),
    )(zp, pad2(w1, P, P), pad2(b1[None], 1, P), pad2(w2, P, P), pad2(b2[None], 1, P), pad2(w3, P, P),
      pad2(b3[None], 1, P), pad2(w4, P, 2 * db), pad2(freq[None], 1, P), deltas)
    L2 = L // 2
    fb = min(LCONV_BINS, L2)
    nfb = L2 // fb
    blk = pl.BlockSpec((fb, db), lambda j: (j, 0))
    return pl.pallas_call(
        _spec_kernel,
        grid=(nfb,),
        in_specs=[pl.BlockSpec((L, 4 * db), lambda j: (0, 0)),
                  pl.BlockSpec((fb, L2), lambda j: (j, 0)), pl.BlockSpec((fb, L2), lambda j: (nfb + j, 0)), blk, blk],
        out_specs=[blk] * 4,
        out_shape=[jax.ShapeDtypeStruct((L2, db), F32)] * 4,
        compiler_params=_cparams(("parallel",)),
    )(sd, wf, wf, cw, sw)


def _inproj_kernel(x_ref, g_ref, w_ref, b_ref, a_ref, hy_ref, gate_ref, *, da, db):
    h = _rms(x_ref[...], g_ref[...]).astype(BF16)
    c0, c1, c2 = da, 2 * da, 2 * da + 3 * db
    za = _dot(h, w_ref[:, 0:c0]) + b_ref[:, 0:c0]
    zg = _dot(h, w_ref[:, c0:c1]) + b_ref[:, c0:c1]
    a_ref[...] = (za * jax.nn.sigmoid(zg)).astype(BF16)
    hy_ref[...] = (_dot(h, w_ref[:, c1:c2]) + b_ref[:, c1:c2]).astype(BF16)
    gate_ref[...] = jax.nn.sigmoid(_dot(h, w_ref[:, c2:]) + b_ref[:, c2:]).astype(BF16)


def _inproj(x, g, w, b, da, db):
    T, D = x.shape
    C = w.shape[1]
    tm = min(TOKEN_TILE, T)
    ng = C - 2 * da - 3 * db
    row = lambda i: (i, 0)
    fix = lambda i: (0, 0)
    return pl.pallas_call(
        functools.partial(_inproj_kernel, da=da, db=db),
        grid=(T // tm,),
        in_specs=[pl.BlockSpec((tm, D), row), pl.BlockSpec((1, D), fix), pl.BlockSpec((D, C), fix),
                  pl.BlockSpec((1, C), fix)],
        out_specs=[pl.BlockSpec((tm, da), row), pl.BlockSpec((tm, 3 * db), row), pl.BlockSpec((tm, ng), row)],
        out_shape=[jax.ShapeDtypeStruct((T, da), BF16), jax.ShapeDtypeStruct((T, 3 * db), BF16),
                   jax.ShapeDtypeStruct((T, ng), BF16)],
        compiler_params=_cparams(("parallel",)),
    )(x, g, w, b)


def _conv_kernel(a_ref, hy_ref, cw_ref, cb_ref, lg_ref, lb_ref, sw_ref, sb_ref,
                 act_ref, x0_ref, vv_ref, pa_ref, ph_ref, sh_ref, cv_ref, *, L, da, db):
    K = cw_ref.shape[0]
    KS = sw_ref.shape[0]
    R = CONV_ROWS
    P = CONV_PAD
    LANES = 128
    zeros_a = jnp.zeros((P, da), F32)
    zeros_h = jnp.zeros((P, 3 * db), F32)
    pa_ref[0:P, :] = zeros_a
    pa_ref[P + L:P + L + P, :] = zeros_a
    ph_ref[0:P, :] = zeros_h
    ph_ref[P + L:P + L + P, :] = zeros_h
    pa_ref[P:P + L, :] = a_ref[0].astype(F32)
    ph_ref[P:P + L, :] = hy_ref[0].astype(F32)

    def chunk(c, carry):
        base = pl.multiple_of(c * R, R)
        for cb in range(da // LANES):
            cols = slice(cb * LANES, (cb + 1) * LANES)
            win = pa_ref[pl.ds(base, R + 2 * P), cols]
            for r in range(8):
                sh_ref[r] = win[r:r + R + 2 * P - 8, :]
            acc = jnp.zeros((R, LANES), F32) + cb_ref[:, cols]
            for k in range(K):
                q, r = divmod(P - K // 2 + k, 8)
                acc = acc + cw_ref[k:k + 1, cols] * sh_ref[r, 8 * q:8 * q + R, :]
            cv_ref[:, cols] = acc
        acc = cv_ref[...]
        mu = jnp.mean(acc, axis=-1, keepdims=True)
        xc = acc - mu
        var = jnp.mean(xc * xc, axis=-1, keepdims=True)
        y = xc * lax.rsqrt(var + LN_EPS) * lg_ref[...] + lb_ref[...]
        act_ref[0, pl.ds(base, R), :] = (y * jax.nn.sigmoid(y)).astype(BF16)

        def short(lo):
            cols = slice(lo, lo + LANES)
            win = ph_ref[pl.ds(base, R + 2 * P), cols]
            u = jnp.zeros((R, LANES), F32) + sb_ref[:, cols]
            for k in range(KS):
                o = P - KS // 2 + k
                u = u + sw_ref[k:k + 1, cols] * win[o:o + R, :]
            return u

        for cb in range(db // LANES):
            lo = cb * LANES
            cols = slice(lo, lo + LANES)
            vv_ref[0, pl.ds(base, R), cols] = (short(db + lo) * short(2 * db + lo)).astype(BF16)
            x0_ref[0, pl.ds(base, R), cols] = short(lo).astype(BF16)
        return carry

    lax.fori_loop(0, L // R, chunk, 0)


def _convs(a, hy, cw, cb, lg, lb, sw, sb):
    B, L, da = a.shape
    db = hy.shape[2] // 3
    bat = lambda b: (b, 0, 0)
    fix = lambda b: (0, 0)
    return pl.pallas_call(
        functools.partial(_conv_kernel, L=L, da=da, db=db),
        grid=(B,),
        in_specs=[pl.BlockSpec((1, L, da), bat), pl.BlockSpec((1, L, 3 * db), bat),
                  pl.BlockSpec(cw.shape, fix), pl.BlockSpec((1, da), fix), pl.BlockSpec((1, da), fix),
                  pl.BlockSpec((1, da), fix), pl.BlockSpec(sw.shape, fix), pl.BlockSpec((1, 3 * db), fix)],
        out_specs=[pl.BlockSpec((1, L, da), bat), pl.BlockSpec((1, L, db), bat), pl.BlockSpec((1, L, db), bat)],
        out_shape=[jax.ShapeDtypeStruct((B, L, da), BF16), jax.ShapeDtypeStruct((B, L, db), BF16),
                   jax.ShapeDtypeStruct((B, L, db), BF16)],
        scratch_shapes=[pltpu.VMEM((L + 2 * CONV_PAD, da), F32), pltpu.VMEM((L + 2 * CONV_PAD, 3 * db), F32),
                        pltpu.VMEM((8, CONV_ROWS + 2 * CONV_PAD - 8, 128), F32), pltpu.VMEM((CONV_ROWS, da), F32)],
        compiler_params=_cparams(("parallel",)),
    )(a, hy, cw, cb, lg, lb, sw, sb)


def _lconv_kernel(v_ref, x0_ref, wfr_ref, wfi_ref, cw_ref, sw_ref, kare_ref, kaim_ref, kbre_ref, kbim_ref,
                  wir_ref, wii_ref, skip_ref, u_ref, xe_ref, xo_ref, ye_ref, yo_ref, *, nfb, fb):
    j = pl.program_id(1)

    @pl.when(j == 0)
    def _():
        xe_ref[...], xo_ref[...] = _split_rows(v_ref[0])
        ye_ref[...] = jnp.zeros(ye_ref.shape, F32)
        yo_ref[...] = jnp.zeros(yo_ref.shape, F32)

    row0 = _first_row(fb, j)
    wfr, wfi, wir, wii = wfr_ref[...], wfi_ref[...], wir_ref[...], wii_ref[...]
    G = min(LCONV_LANES, v_ref.shape[2])
    for g in range(v_ref.shape[2] // G):
        cols = slice(g * G, (g + 1) * G)
        cw = cw_ref[:, cols]
        sw = sw_ref[:, cols]
        are, aim, bre, bim = _half_spectra(xe_ref[:, cols], xo_ref[:, cols], wfr, wfi, cw, sw, row0)
        kare, kaim, kbre, kbim = kare_ref[:, cols], kaim_ref[:, cols], kbre_ref[:, cols], kbim_ref[:, cols]
        a, b, c, d = are * kare, aim * kaim, bre * kbre, bim * kbim
        pare = a - jnp.where(row0, 0.0, b)
        paim = jnp.where(row0, b - d, are * kaim + aim * kare)
        pbre = c - jnp.where(row0, 0.0, d)
        pbim = jnp.where(row0, aim * kbim + bim * kaim, bre * kbim + bim * kbre)
        q0re = pare + pbre
        q0im = jnp.where(row0, 2.0 * paim, paim - pbim)
        dre = pare - pbre
        dim = paim + pbim
        q1re = dre * cw - dim * sw
        q1im = jnp.where(row0, -2.0 * pbim, dre * sw + dim * cw)
        ye_ref[:, cols] += _dot(wir, q0re.astype(BF16)) + _dot(wii, q0im.astype(BF16))
        yo_ref[:, cols] += _dot(wir, q1re.astype(BF16)) + _dot(wii, q1im.astype(BF16))

    @pl.when(j == nfb - 1)
    def _():
        skip = skip_ref[...]
        x0e, x0o = _split_rows(x0_ref[0])
        oe = x0e.astype(F32) * (0.5 * ye_ref[...] + skip * xe_ref[...].astype(F32))
        oo = x0o.astype(F32) * (0.5 * yo_ref[...] + skip * xo_ref[...].astype(F32))
        u_ref[0] = _merge_rows(oe, oo)


def _long_conv(vv, x0, wf, wi, cw, sw, kspec, skip):
    B, L, db = vv.shape
    L2 = L // 2
    fb = min(LCONV_BINS, L2)
    nfb = L2 // fb
    bat = lambda b, j: (b, 0, 0)
    blk = pl.BlockSpec((fb, db), lambda b, j: (j, 0))
    return pl.pallas_call(
        functools.partial(_lconv_kernel, nfb=nfb, fb=fb),
        grid=(B, nfb),
        in_specs=[pl.BlockSpec((1, L, db), bat), pl.BlockSpec((1, L, db), bat),
                  pl.BlockSpec((fb, L2), lambda b, j: (j, 0)), pl.BlockSpec((fb, L2), lambda b, j: (nfb + j, 0)),
                  blk, blk, blk, blk, blk, blk,
                  pl.BlockSpec((L2, fb), lambda b, j: (0, j)), pl.BlockSpec((L2, fb), lambda b, j: (0, nfb + j)),
                  pl.BlockSpec((1, db), lambda b, j: (0, 0))],
        out_specs=pl.BlockSpec((1, L, db), bat),
        out_shape=jax.ShapeDtypeStruct((B, L, db), BF16),
        scratch_shapes=[pltpu.VMEM((L2, db), BF16), pltpu.VMEM((L2, db), BF16),
                        pltpu.VMEM((L2, db), F32), pltpu.VMEM((L2, db), F32)],
        compiler_params=_cparams(("parallel", "arbitrary")),
    )(vv, x0, wf, wf, cw, sw, *kspec, wi, wi, skip)


def _post_kernel(x_ref, act_ref, u_ref, gate_ref, wa_ref, ba_ref, wb_ref, bb_ref, wo_ref, n2_ref, rw_ref,
                 x1_ref, tok_ref, aff_ref, *, E):
    tm, D = x_ref.shape
    RS = min(POST_ROWS, tm)
    for r0 in range(0, tm, RS):
        rows = slice(r0, r0 + RS)
        ya = _dot(act_ref[rows, :], wa_ref[...]) + ba_ref[...]
        yb = _dot(u_ref[rows, :], wb_ref[...]) + bb_ref[...]
        m = gate_ref[rows, :D].astype(F32) * ya + gate_ref[rows, D:].astype(F32) * yb
        x1 = x_ref[rows, :] + _dot(m.astype(BF16), wo_ref[...])
        x1_ref[rows, :] = x1
        h2 = _rms(x1, n2_ref[...]).astype(BF16)
        tok_ref[rows, :D // 2] = _pack_rows(h2)
        lane = lax.broadcasted_iota(I32, (RS, SIDE_LANES), 1)
        is_expert = lane < E
        logits = jnp.where(is_expert, _dot(h2, rw_ref[...]), -1e30)
        ex = jnp.exp(logits - jnp.max(logits, axis=1, keepdims=True))
        aff = jnp.where(is_expert, ex / jnp.sum(ex, axis=1, keepdims=True), 0.0)
        tok_id = (pl.program_id(0) * tm + r0) % COMBINE_TILE + lax.broadcasted_iota(I32, (RS, SIDE_LANES), 0)
        tok_ref[rows, D // 2:] = jnp.where(lane == E, tok_id, lax.bitcast_convert_type(aff, I32))
        aff_ref[:, rows] = aff.T[0:E, :]


def _post(x, act, u, gate, wa, ba, wb, bb, wo, n2, rw, E):
    T, D = x.shape
    tm = min(TOKEN_TILE, T)
    assert COMBINE_TILE % tm == 0
    RW = D // 2 + SIDE_LANES
    row = lambda i: (i, 0)
    fix = lambda i: (0, 0)
    full = lambda a: pl.BlockSpec(a.shape, fix)
    return pl.pallas_call(
        functools.partial(_post_kernel, E=E),
        grid=(T // tm,),
        in_specs=[pl.BlockSpec((tm, D), row), pl.BlockSpec((tm, act.shape[1]), row),
                  pl.BlockSpec((tm, u.shape[1]), row), pl.BlockSpec((tm, gate.shape[1]), row),
                  full(wa), full(ba), full(wb), full(bb), full(wo), full(n2), full(rw)],
        out_specs=[pl.BlockSpec((tm, D), row), pl.BlockSpec((tm, RW), row), pl.BlockSpec((E, tm), lambda i: (0, i))],
        out_shape=[jax.ShapeDtypeStruct((T, D), F32), jax.ShapeDtypeStruct((T, RW), I32),
                   jax.ShapeDtypeStruct((E, T), F32)],
        compiler_params=_cparams(("parallel",)),
    )(x, act, u, gate, wa, ba, wb, bb, wo, n2, rw)


def _select_kernel(a_ref, idx_ref, ntab_ref, otab_ref, thr_ref, *, cap):
    E, nt, W = a_ref.shape
    nsb = cap // W
    bits = lax.bitcast_convert_type(a_ref[...], I32)

    def bisect(b, cur):
        cand = cur | jnp.left_shift(jnp.int32(1), 30 - b)
        hit = jnp.where(bits >= cand, 1.0, 0.0)
        cnt = jnp.sum(jnp.sum(hit, axis=2, keepdims=True), axis=1, keepdims=True)
        return jnp.where(cnt >= cap, cand, cur)

    thr_ref[...] = lax.fori_loop(0, 31, bisect, jnp.zeros((E, 1, 1), I32))

    r_i = lax.broadcasted_iota(I32, (W, W), 0)
    c_i = lax.broadcasted_iota(I32, (W, W), 1)
    upper = jnp.where(r_i <= c_i, 1.0, 0.0).astype(BF16)
    ones = jnp.ones((W, W), BF16)
    tr = lax.broadcasted_iota(I32, (nt, nt), 0)
    tc = lax.broadcasted_iota(I32, (nt, nt), 1)
    lower = jnp.where(tc < tr, 1.0, 0.0).astype(BF16)
    tile_id = lax.broadcasted_iota(I32, (nt, W), 0).astype(F32)
    lane = lax.broadcasted_iota(I32, (1, W), 1).astype(F32)

    def prefix(maskf):
        mb = maskf.astype(BF16)
        incl = _dot(mb, upper)
        tot = _dot(mb, ones)
        off = _dot(lower, tot.astype(BF16))
        return incl, tot, off

    def expert(e, carry):
        ae = a_ref[e]
        be = lax.bitcast_convert_type(ae, I32)
        th = thr_ref[e]
        gt = be > th
        eq = be == th
        gtf = jnp.where(gt, 1.0, 0.0)
        eqf = jnp.where(eq, 1.0, 0.0)
        need = cap - jnp.sum(jnp.sum(gtf, axis=1, keepdims=True), axis=0, keepdims=True)
        e_incl, _, e_off = prefix(eqf)
        eq_rank = e_off + e_incl - eqf
        sel = gt | (eq & (eq_rank < need))
        self_ = jnp.where(sel, 1.0, 0.0)
        incl, tot, off = prefix(self_)
        ntab_ref[e] = tot[:, 0:128]
        otab_ref[e] = off[:, 0:128]
        incl_t = incl.T.astype(BF16)
        off_hi = off + tot

        def slots(c, carry2):
            s_row = lane + jnp.asarray(c * W, F32)
            oh = jnp.where((off <= s_row) & (s_row < off_hi), 1.0, 0.0)
            tile = jnp.sum(oh * tile_id, axis=0, keepdims=True)
            rank = s_row - jnp.sum(oh * off, axis=0, keepdims=True)
            rt = _dot(incl_t, oh.astype(BF16))
            pos = jnp.sum(jnp.where(rt <= rank, 1.0, 0.0), axis=0, keepdims=True)
            idx_ref[e, pl.ds(c, 1), :] = (tile * W + pos).astype(I32)
            return carry2

        lax.fori_loop(0, nsb, slots, 0)
        return carry

    lax.fori_loop(0, E, expert, 0)


def _select(aff3, cap):
    E, nt, W = aff3.shape
    return pl.pallas_call(
        functools.partial(_select_kernel, cap=cap),
        out_shape=[jax.ShapeDtypeStruct((E, cap // W, W), I32), jax.ShapeDtypeStruct((E, nt, 128), F32),
                   jax.ShapeDtypeStruct((E, nt, 128), F32)],
        scratch_shapes=[pltpu.VMEM((E, 1, 1), I32)],
        compiler_params=pltpu.CompilerParams(vmem_limit_bytes=VMEM_LIMIT),
    )(aff3)


SC_GATHER_ROWS = 128


def _sc_gather(table, idx):
    M = idx.shape[0]
    Dw = table.shape[1]
    win = SC_GATHER_ROWS
    info = plsc.get_sparse_core_info()
    nc = info.num_cores
    nw = nc * info.num_subcores
    per_w = M // nw
    assert M % (nw * win) == 0
    mesh = plsc.VectorSubcoreMesh(core_axis_name="c", subcore_axis_name="s")

    @functools.partial(
        pl.kernel, mesh=mesh,
        out_type=jax.ShapeDtypeStruct((M, Dw), table.dtype),
        scratch_types=[pltpu.VMEM((win,), I32), pltpu.VMEM((win, Dw), table.dtype), pltpu.SemaphoreType.DMA],
    )
    def gather(table_hbm, idx_hbm, out_hbm, idx_v, rows_v, sem):
        wid = lax.axis_index("s") * nc + lax.axis_index("c")

        @pl.loop(0, per_w // win)
        def _(j):
            base = pl.multiple_of(wid * per_w + j * win, win)
            pltpu.sync_copy(idx_hbm.at[pl.ds(base, win)], idx_v)
            pltpu.async_copy(table_hbm.at[idx_v], rows_v, sem).wait()
            pltpu.sync_copy(rows_v, out_hbm.at[pl.ds(base, win)])

    return gather(table, idx)


def _ffn_kernel(xs_ref, wgu_ref, wd_ref, ye_ref):
    R = xs_ref.shape[0]
    half = wgu_ref.shape[1] // 2
    F = wd_ref.shape[1]
    xs = _unpack_rows(xs_ref[:, :half])
    side = xs_ref[:, half:]
    lane = lax.broadcasted_iota(I32, (R, SIDE_LANES), 1)
    aff = lax.bitcast_convert_type(side, F32)
    gate = jnp.sum(jnp.where(lane == pl.program_id(0), aff, 0.0), axis=1, keepdims=True)
    gu = _dot(xs, wgu_ref[0])
    g = gu[:, :F]
    hid = (g * jax.nn.sigmoid(g) * gu[:, F:]).astype(BF16)
    ye = _dot(hid, wd_ref[0]) * gate
    ye_ref[:, :half] = _pack_rows(ye.astype(BF16))
    ye_ref[:, half:] = side


def _expert_ffn(xs, wgu, wd):
    E, F, D = wd.shape
    cap = xs.shape[0] // E
    R = min(FFN_ROWS, cap)
    assert cap % R == 0 and wgu.shape == (E, D, 2 * F)
    RW = xs.shape[1]
    nsb = cap // R
    return pl.pallas_call(
        _ffn_kernel,
        grid=(E, nsb),
        in_specs=[pl.BlockSpec((R, RW), lambda e, s: (e * nsb + s, 0)),
                  pl.BlockSpec((1, D, 2 * F), lambda e, s: (e, 0, 0)),
                  pl.BlockSpec((1, F, D), lambda e, s: (e, 0, 0))],
        out_specs=pl.BlockSpec((R, RW), lambda e, s: (e * nsb + s, 0)),
        out_shape=jax.ShapeDtypeStruct(xs.shape, I32),
        compiler_params=_cparams(("parallel", "parallel")),
    )(xs, wgu, wd)


def _combine_tables(ntab, otab, cap):
    E, nt = ntab.shape
    W = COMBINE_ROWS
    g = COMBINE_TILE // ROUTE_TILE
    ntc = nt // g
    n2 = ntab.reshape(E, ntc, g).sum(-1)
    off2 = otab.reshape(E, ntc, g)[:, :, 0]
    hi = jnp.cumsum(n2, axis=0)
    lo = hi - n2
    m = hi[-1]
    nch = jnp.maximum(1, (m + W - 1) // W)
    cb_hi = jnp.cumsum(nch)
    cb_lo = cb_hi - nch
    nc_max = (E * cap) // W + ntc
    c = jnp.arange(nc_max, dtype=I32)
    tile = jnp.minimum(jnp.sum((cb_hi[None, :] <= c[:, None]).astype(I32), axis=1), ntc - 1)
    k = jnp.where(c < cb_hi[-1], c - cb_lo[tile], nch[ntc - 1] + c - cb_hi[-1])
    delta = jnp.arange(E, dtype=I32)[:, None] * cap + off2 - lo
    bcast = lambda a: jnp.broadcast_to(a.T.astype(F32)[:, :, None], (ntc, E, 128))
    blk = jnp.minimum(c, cb_hi[-1] - 1)
    sched = (tile, k.astype(I32), m.astype(I32), nch.astype(I32), blk.astype(I32))
    return sched, bcast(lo), bcast(hi), bcast(delta)


def _plan_kernel(tile_s, k_s, lo_ref, hi_ref, dl_ref, src_ref, *, n_rows):
    W = src_ref.shape[1]
    lane = lax.broadcasted_iota(I32, (1, W), 1).astype(F32)

    def chunk(c, carry):
        t = tile_s[c]
        q = lane + jnp.asarray(k_s[c] * W, F32)
        lo = lo_ref[t][:, 0:1]
        hi = hi_ref[t][:, 0:1]
        mine = (lo <= q) & (q < hi)
        src = jnp.sum(jnp.where(mine, dl_ref[t][:, 0:1] + q, 0.0), axis=0, keepdims=True)
        used = jnp.sum(jnp.where(mine, 1.0, 0.0), axis=0, keepdims=True) > 0.0
        p = lane + jnp.asarray(c * W, F32)
        spare = jnp.where(p < n_rows, p, p - n_rows)
        src_ref[pl.ds(c, 1), :] = jnp.where(used, src, spare).astype(I32)
        return carry

    lax.fori_loop(0, src_ref.shape[0], chunk, 0)


def _combine_plan(tile, k, lo, hi, delta, n_rows):
    smem = pl.BlockSpec(memory_space=pltpu.SMEM)
    vmem = pl.BlockSpec(memory_space=pltpu.VMEM)
    assert tile.shape[0] * COMBINE_ROWS <= 2 * n_rows
    return pl.pallas_call(
        functools.partial(_plan_kernel, n_rows=n_rows),
        in_specs=[smem, smem, vmem, vmem, vmem],
        out_specs=vmem,
        out_shape=jax.ShapeDtypeStruct((tile.shape[0], COMBINE_ROWS), I32),
        compiler_params=pltpu.CompilerParams(vmem_limit_bytes=VMEM_LIMIT),
    )(tile, k, lo, hi, delta)


def _combine_kernel(tile_s, k_s, m_s, nch_s, blk_s, x1_ref, rows_ref, g_ref, y_ref, acc_ref, *, E):
    c = pl.program_id(0)
    t = tile_s[c]
    k = k_s[c]
    W = rows_ref.shape[0]
    CT, D = x1_ref.shape

    @pl.when(k == 0)
    def _():
        acc_ref[...] = x1_ref[...]

    @pl.when(k < nch_s[t])
    def _():
        rows = _unpack_rows(rows_ref[:, :D // 2])
        side_t = rows_ref[:, D // 2:].astype(F32).T
        tok = side_t[E:E + 1, :]
        q = k * W + lax.broadcasted_iota(I32, (1, W), 1)
        hit = (tok == lax.broadcasted_iota(I32, (CT, W), 0).astype(F32)) & (q < m_s[t])
        onehot = jnp.where(hit, 1.0, 0.0).astype(BF16)
        acc_ref[...] += _dot(onehot, rows)

    @pl.when(k == nch_s[t] - 1)
    def _():
        y_ref[...] = _rms(acc_ref[...], g_ref[...])


def _combine(sched, x1, rows, g, E):
    T, D = x1.shape
    W = COMBINE_ROWS
    CT = COMBINE_TILE
    RW = rows.shape[1]
    grid_spec = pltpu.PrefetchScalarGridSpec(
        num_scalar_prefetch=len(sched),
        grid=(sched[0].shape[0],),
        in_specs=[pl.BlockSpec((CT, D), lambda c, tile_s, *_: (tile_s[c], 0)),
                  pl.BlockSpec((W, RW), lambda c, tile_s, k_s, m_s, nch_s, blk_s: (blk_s[c], 0)),
                  pl.BlockSpec((1, D), lambda c, *_: (0, 0))],
        out_specs=pl.BlockSpec((CT, D), lambda c, tile_s, *_: (tile_s[c], 0)),
        scratch_shapes=[pltpu.VMEM((CT, D), F32)],
    )
    return pl.pallas_call(
        functools.partial(_combine_kernel, E=E),
        grid_spec=grid_spec,
        out_shape=jax.ShapeDtypeStruct((T, D), F32),
        compiler_params=_cparams(("arbitrary",)),
    )(*sched, x1, rows, g)


def _layer(x, p, spec_cache):
    B, L, D = x.shape
    T = B * L
    da = p["conv_dw_w"].shape[1]
    db = p["hy_skip"].shape[1]
    E = p["n_experts"]
    W = ROUTE_TILE
    cap = max(1, min(T, CAPACITY_FACTOR * T // E))
    assert T % COMBINE_TILE == 0 and cap % W == 0 and L % CONV_ROWS == 0 and E < SIDE_LANES

    if L not in spec_cache:
        wf, wi = _dft_tables(L // 2)
        cw, sw = _twiddles(L, db)
        kspec = _filter_spectrum(L, p["hy_f_w1"], p["hy_f_b1"], p["hy_f_w2"], p["hy_f_b2"], p["hy_f_w3"],
                                 p["hy_f_b3"], p["hy_f_w4"], p["hy_f_freq"], wf, cw, sw)
        spec_cache[L] = (wf, wi, cw, sw, kspec)
    wf, wi, cw, sw, kspec = spec_cache[L]

    x2 = x.reshape(T, D)
    a, hy, gate = _inproj(x2, p["norm1_g"], p["w_in"], p["b_in"], da, db)
    act, x0, vv = _convs(a.reshape(B, L, da), hy.reshape(B, L, 3 * db), p["conv_dw_w"], p["conv_dw_b"],
                         p["conv_ln_g"], p["conv_ln_b"], p["hy_short_w"], p["hy_short_b"])
    u = _long_conv(vv, x0, wf, wi, cw, sw, kspec, p["hy_skip"])
    x1, tok, aff = _post(x2, act.reshape(T, da), u.reshape(T, db), gate, p["conv_w_out"], p["conv_b_out"],
                         p["hy_w_out"], p["hy_b_out"], p["w_o"], p["norm2_g"], p["router_w"], E)
    nt = T // W
    idx, ntab, otab = _select(aff.reshape(E, nt, W), cap)
    xs = _sc_gather(tok, idx.reshape(E * cap))
    ye = _expert_ffn(xs, p["exp_w_gate_up"], p["exp_w_down"])
    sched, lo, hi, delta = _combine_tables(ntab[:, :, 0].astype(I32), otab[:, :, 0].astype(I32), cap)
    src = _combine_plan(sched[0], sched[1], lo, hi, delta, E * cap)
    rows = _sc_gather(ye, src.reshape(-1))
    y = _combine(sched, x1, rows, p["norm_f_g"], E)
    return y.reshape(B, L, D)


def kernel(x_prompt, x_sample, norm1_g, w_in, b_in, conv_dw_w, conv_dw_b, conv_ln_g, conv_ln_b, conv_w_out, conv_b_out, hy_short_w, hy_short_b, hy_f_w1, hy_f_b1, hy_f_w2, hy_f_b2, hy_f_w3, hy_f_b3, hy_f_w4, hy_f_freq, hy_skip, hy_w_out, hy_b_out, w_o, norm2_g, router_w, exp_w_gate, exp_w_up, exp_w_down, norm_f_g):
    depth = w_in.shape[0]
    assert depth == 1
    i = 0
    row = lambda a: a[i][None].astype(F32)
    p = {
        "norm1_g": row(norm1_g), "w_in": w_in[i].astype(BF16), "b_in": row(b_in),
        "conv_dw_w": conv_dw_w[i].astype(F32), "conv_dw_b": row(conv_dw_b),
        "conv_ln_g": row(conv_ln_g), "conv_ln_b": row(conv_ln_b),
        "conv_w_out": conv_w_out[i].astype(BF16), "conv_b_out": row(conv_b_out),
        "hy_short_w": hy_short_w[i].astype(F32), "hy_short_b": row(hy_short_b),
        "hy_f_w1": hy_f_w1[i], "hy_f_b1": hy_f_b1[i], "hy_f_w2": hy_f_w2[i], "hy_f_b2": hy_f_b2[i],
        "hy_f_w3": hy_f_w3[i], "hy_f_b3": hy_f_b3[i], "hy_f_w4": hy_f_w4[i], "hy_f_freq": hy_f_freq[i],
        "hy_skip": row(hy_skip), "hy_w_out": hy_w_out[i].astype(BF16), "hy_b_out": row(hy_b_out),
        "w_o": w_o[i].astype(BF16), "norm2_g": row(norm2_g), "n_experts": router_w.shape[2],
        "router_w": jnp.pad(router_w[i], ((0, 0), (0, SIDE_LANES - router_w.shape[2]))).astype(BF16),
        "exp_w_gate_up": jnp.concatenate([exp_w_gate[i].astype(BF16), exp_w_up[i].astype(BF16)], axis=2),
        "exp_w_down": exp_w_down[i].astype(BF16), "norm_f_g": norm_f_g[None].astype(F32),
    }
    spec_cache = {}
    y_sample = _layer(x_sample, p, spec_cache)
    y_prompt = _layer(x_prompt, p, spec_cache)
    return (y_prompt, y_sample)
```

```python
import functools
import math

import jax
import jax.numpy as jnp
from jax import lax
from jax.experimental import pallas as pl
from jax.experimental.pallas import tpu as pltpu
from jax.experimental.pallas import tpu_sc as plsc

F32 = jnp.float32
BF16 = jnp.bfloat16
I32 = jnp.int32

NORM_EPS = 1e-6
LN_EPS = 1e-5
FAST_DECAY_PCT = 0.3
SLOW_DECAY_PCT = 1.5
DECAY_TARGET = 1e-2
CAPACITY_FACTOR = 2

ROUTE_TILE = 256
TOKEN_TILE = 512
CONV_ROWS = 128
CONV_PAD = 16
POST_ROWS = 256
LCONV_BINS = 512
LCONV_LANES = 256
FFN_ROWS = 1024
COMBINE_TILE = 512
COMBINE_ROWS = 512
SIDE_LANES = 128
VMEM_LIMIT = 56 * 1024 * 1024


def _cparams(sem):
    return pltpu.CompilerParams(dimension_semantics=sem, vmem_limit_bytes=VMEM_LIMIT)


def _dot(a, b):
    return jnp.dot(a, b, preferred_element_type=F32)


def _dot_nt(a, b, precision=None):
    return lax.dot_general(a, b, (((1,), (1,)), ((), ())), preferred_element_type=F32, precision=precision)


def _rms(x, g):
    return x * lax.rsqrt(jnp.mean(x * x, axis=-1, keepdims=True) + NORM_EPS) * g


_HI_MASK = -65536


def _pack_rows(xb):
    half = xb.shape[1] // 2
    lo = lax.bitcast_convert_type(xb[:, :half].astype(F32), I32)
    hi = lax.bitcast_convert_type(xb[:, half:].astype(F32), I32)
    return (hi & _HI_MASK) | lax.shift_right_logical(lo, jnp.full(lo.shape, 16, I32))


def _unpack_rows(w):
    lo = lax.bitcast_convert_type(lax.shift_left(w, jnp.full(w.shape, 16, I32)), F32)
    hi = lax.bitcast_convert_type(w & _HI_MASK, F32)
    return jnp.concatenate([lo, hi], axis=1).astype(BF16)


def _filter_kernel(z_ref, w1_ref, b1_ref, w2_ref, b2_ref, w3_ref, b3_ref, w4_ref, fr_ref, dl_ref, sd_ref):
    hp = lax.Precision.HIGHEST
    fr = fr_ref[...]
    z = z_ref[...]
    h = jnp.sin(fr * (jnp.dot(z, w1_ref[...], precision=hp, preferred_element_type=F32) + b1_ref[...]))
    h = jnp.sin(fr * (jnp.dot(h, w2_ref[...], precision=hp, preferred_element_type=F32) + b2_ref[...]))
    h = jnp.sin(fr * (jnp.dot(h, w3_ref[...], precision=hp, preferred_element_type=F32) + b3_ref[...]))
    h4 = jnp.dot(h, w4_ref[...], precision=hp, preferred_element_type=F32)
    L = z.shape[0]
    db = dl_ref.shape[1]
    t = z[:, 0:1]
    decay = jnp.exp(-t * dl_ref[...])
    hf = h4[:, :db] * decay
    hb = h4[:, db:] * decay
    norm = jnp.sum(jnp.abs(hf), axis=0, keepdims=True) + jnp.sum(jnp.abs(hb), axis=0, keepdims=True)
    hf = hf / norm
    hb = hb / norm
    row = lax.broadcasted_iota(I32, (L, 1), 0)
    hbs = jnp.where(row == 0, 0.0, pltpu.roll(hb, 1, axis=0))
    s = hf + hbs
    d = hf - hbs
    s_hi = s.astype(BF16)
    d_hi = d.astype(BF16)
    sd_ref[:, 0 * db:1 * db] = s_hi
    sd_ref[:, 1 * db:2 * db] = (s - s_hi.astype(F32)).astype(BF16)
    sd_ref[:, 2 * db:3 * db] = d_hi
    sd_ref[:, 3 * db:4 * db] = (d - d_hi.astype(F32)).astype(BF16)


def _split_rows(x):
    w = pltpu.bitcast(x, I32)
    even = lax.bitcast_convert_type(lax.shift_left(w, jnp.full(w.shape, 16, I32)), F32).astype(BF16)
    odd = lax.bitcast_convert_type(w & _HI_MASK, F32).astype(BF16)
    return even, odd


def _merge_rows(even, odd):
    e = lax.bitcast_convert_type(even.astype(BF16).astype(F32), I32)
    o = lax.bitcast_convert_type(odd.astype(BF16).astype(F32), I32)
    return pltpu.bitcast((o & _HI_MASK) | lax.shift_right_logical(e, jnp.full(e.shape, 16, I32)), BF16)


def _half_spectra(xe, xo, wfr, wfi, cw, sw, row0):
    ere = _dot(wfr, xe)
    eim = _dot(wfi, xe)
    ore = _dot(wfr, xo)
    oim = _dot(wfi, xo)
    tre = cw * ore + sw * oim
    tim = cw * oim - sw * ore
    return ere + tre, jnp.where(row0, eim, eim + tim), ere - tre, jnp.where(row0, -oim, tim - eim)


def _first_row(fb, block):
    return (lax.broadcasted_iota(I32, (fb, 1), 0) + block * fb) == 0


def _spec_kernel(sd_ref, wfr_ref, wfi_ref, cw_ref, sw_ref, kare_ref, kaim_ref, kbre_ref, kbim_ref):
    fb, db = kare_ref.shape
    row0 = _first_row(fb, pl.program_id(0))
    xe, xo = _split_rows(sd_ref[...])

    def spectra(g):
        cols = slice(g * db, (g + 1) * db)
        return _half_spectra(xe[:, cols], xo[:, cols], wfr_ref[...], wfi_ref[...], cw_ref[...], sw_ref[...], row0)

    s_hi, s_lo, d_hi, d_lo = spectra(0), spectra(1), spectra(2), spectra(3)
    kare_ref[...] = s_hi[0] + s_lo[0]
    kbre_ref[...] = s_hi[2] + s_lo[2]
    kaim_ref[...] = jnp.where(row0, s_hi[1] + s_lo[1], d_hi[1] + d_lo[1])
    kbim_ref[...] = d_hi[3] + d_lo[3]


def _dft_tables(L):
    n2 = 2 * L
    k = jnp.arange(L, dtype=I32)[:, None]
    n = jnp.arange(L, dtype=I32)[None, :]
    ph = ((k * n) % n2).astype(F32) * (2.0 * math.pi / n2)
    c = jnp.cos(ph)
    s = jnp.sin(ph)
    alt = jnp.where(n % 2 == 0, 1.0, -1.0).astype(F32)
    wf_im = jnp.where(k == 0, alt, -s)
    wf = jnp.concatenate([c, wf_im], axis=0).astype(BF16)
    ck = jnp.where(k == 0, 1.0, 2.0) / n2
    wi_re = (c * ck).T
    wi_im = jnp.where(k == 0, alt / n2, -s * ck).T
    wi = jnp.concatenate([wi_re, wi_im], axis=1).astype(BF16)
    return wf, wi


def _twiddles(L, db):
    ph = jnp.arange(L // 2, dtype=F32)[:, None] * (math.pi / L)
    return jnp.broadcast_to(jnp.cos(ph), (L // 2, db)), jnp.broadcast_to(jnp.sin(ph), (L // 2, db))


def _filter_spectrum(L, w1, b1, w2, b2, w3, b3, w4, freq, wf, cw, sw):
    pe, ffn = w1.shape
    db = w4.shape[1] // 2
    bands = (pe - 1) // 2
    t = jnp.linspace(0.0, 1.0, L, dtype=F32)[:, None]
    w = 2.0 * math.pi * jnp.arange(L, dtype=F32)[:, None] / L
    bnd = jnp.linspace(1e-4, bands - 1, bands, dtype=F32)[None, :]
    z = jnp.concatenate([t, jnp.cos(bnd * w), -jnp.sin(bnd * w)], axis=-1)
    P = 128
    zp = jnp.zeros((L, P), F32).at[:, :pe].set(z)
    pad2 = lambda a, r, c: jnp.zeros((r, c), F32).at[:a.shape[0], :a.shape[1]].set(a.astype(F32))
    max_decay = math.log(DECAY_TARGET) / FAST_DECAY_PCT
    min_decay = math.log(DECAY_TARGET) / SLOW_DECAY_PCT
    deltas = jnp.abs(jnp.linspace(min_decay, max_decay, db, dtype=F32))[None, :]
    sd = pl.pallas_call(
        _filter_kernel,
        out_shape=jax.ShapeDtypeStruct((L, 4 * db), BF16),
        compiler_params=pltpu.CompilerParams(vmem_limit_bytes=VMEM_LIMIT),
    )(zp, pad2(w1, P, P), pad2(b1[None], 1, P), pad2(w2, P, P), pad2(b2[None], 1, P), pad2(w3, P, P),
      pad2(b3[None], 1, P), pad2(w4, P, 2 * db), pad2(freq[None], 1, P), deltas)
    L2 = L // 2
    fb = min(LCONV_BINS, L2)
    nfb = L2 // fb
    blk = pl.BlockSpec((fb, db), lambda j: (j, 0))
    return pl.pallas_call(
        _spec_kernel,
        grid=(nfb,),
        in_specs=[pl.BlockSpec((L, 4 * db), lambda j: (0, 0)),
                  pl.BlockSpec((fb, L2), lambda j: (j, 0)), pl.BlockSpec((fb, L2), lambda j: (nfb + j, 0)), blk, blk],
        out_specs=[blk] * 4,
        out_shape=[jax.ShapeDtypeStruct((L2, db), F32)] * 4,
        compiler_params=_cparams(("parallel",)),
    )(sd, wf, wf, cw, sw)


def _inproj_kernel(x_ref, g_ref, w_ref, b_ref, a_ref, hy_ref, gate_ref, *, da, db):
    h = _rms(x_ref[...], g_ref[...]).astype(BF16)
    c0, c1, c2 = da, 2 * da, 2 * da + 3 * db
    za = _dot(h, w_ref[:, 0:c0]) + b_ref[:, 0:c0]
    zg = _dot(h, w_ref[:, c0:c1]) + b_ref[:, c0:c1]
    a_ref[...] = (za * jax.nn.sigmoid(zg)).astype(BF16)
    hy_ref[...] = (_dot(h, w_ref[:, c1:c2]) + b_ref[:, c1:c2]).astype(BF16)
    gate_ref[...] = jax.nn.sigmoid(_dot(h, w_ref[:, c2:]) + b_ref[:, c2:]).astype(BF16)


def _inproj(x, g, w, b, da, db):
    T, D = x.shape
    C = w.shape[1]
    tm = min(TOKEN_TILE, T)
    ng = C - 2 * da - 3 * db
    row = lambda i: (i, 0)
    fix = lambda i: (0, 0)
    return pl.pallas_call(
        functools.partial(_inproj_kernel, da=da, db=db),
        grid=(T // tm,),
        in_specs=[pl.BlockSpec((tm, D), row), pl.BlockSpec((1, D), fix), pl.BlockSpec((D, C), fix),
                  pl.BlockSpec((1, C), fix)],
        out_specs=[pl.BlockSpec((tm, da), row), pl.BlockSpec((tm, 3 * db), row), pl.BlockSpec((tm, ng), row)],
        out_shape=[jax.ShapeDtypeStruct((T, da), BF16), jax.ShapeDtypeStruct((T, 3 * db), BF16),
                   jax.ShapeDtypeStruct((T, ng), BF16)],
        compiler_params=_cparams(("parallel",)),
    )(x, g, w, b)


def _conv_kernel(a_ref, hy_ref, cw_ref, cb_ref, lg_ref, lb_ref, sw_ref, sb_ref,
                 act_ref, x0_ref, vv_ref, pa_ref, ph_ref, sh_ref, cv_ref, *, L, da, db):
    K = cw_ref.shape[0]
    KS = sw_ref.shape[0]
    R = CONV_ROWS
    P = CONV_PAD
    LANES = 128
    zeros_a = jnp.zeros((P, da), F32)
    zeros_h = jnp.zeros((P, 3 * db), F32)
    pa_ref[0:P, :] = zeros_a
    pa_ref[P + L:P + L + P, :] = zeros_a
    ph_ref[0:P, :] = zeros_h
    ph_ref[P + L:P + L + P, :] = zeros_h
    pa_ref[P:P + L, :] = a_ref[0].astype(F32)
    ph_ref[P:P + L, :] = hy_ref[0].astype(F32)

    def chunk(c, carry):
        base = pl.multiple_of(c * R, R)
        for cb in range(da // LANES):
            cols = slice(cb * LANES, (cb + 1) * LANES)
            win = pa_ref[pl.ds(base, R + 2 * P), cols]
            for r in range(8):
                sh_ref[r] = win[r:r + R + 2 * P - 8, :]
            acc = jnp.zeros((R, LANES), F32) + cb_ref[:, cols]
            for k in range(K):
                q, r = divmod(P - K // 2 + k, 8)
                acc = acc + cw_ref[k:k + 1, cols] * sh_ref[r, 8 * q:8 * q + R, :]
            cv_ref[:, cols] = acc
        acc = cv_ref[...]
        mu = jnp.mean(acc, axis=-1, keepdims=True)
        xc = acc - mu
        var = jnp.mean(xc * xc, axis=-1, keepdims=True)
        y = xc * lax.rsqrt(var + LN_EPS) * lg_ref[...] + lb_ref[...]
        act_ref[0, pl.ds(base, R), :] = (y * jax.nn.sigmoid(y)).astype(BF16)

        def short(lo):
            cols = slice(lo, lo + LANES)
            win = ph_ref[pl.ds(base, R + 2 * P), cols]
            u = jnp.zeros((R, LANES), F32) + sb_ref[:, cols]
            for k in range(KS):
                o = P - KS // 2 + k
                u = u + sw_ref[k:k + 1, cols] * win[o:o + R, :]
            return u

        for cb in range(db // LANES):
            lo = cb * LANES
            cols = slice(lo, lo + LANES)
            vv_ref[0, pl.ds(base, R), cols] = (short(db + lo) * short(2 * db + lo)).astype(BF16)
            x0_ref[0, pl.ds(base, R), cols] = short(lo).astype(BF16)
        return carry

    lax.fori_loop(0, L // R, chunk, 0)


def _convs(a, hy, cw, cb, lg, lb, sw, sb):
    B, L, da = a.shape
    db = hy.shape[2] // 3
    bat = lambda b: (b, 0, 0)
    fix = lambda b: (0, 0)
    return pl.pallas_call(
        functools.partial(_conv_kernel, L=L, da=da, db=db),
        grid=(B,),
        in_specs=[pl.BlockSpec((1, L, da), bat), pl.BlockSpec((1, L, 3 * db), bat),
                  pl.BlockSpec(cw.shape, fix), pl.BlockSpec((1, da), fix), pl.BlockSpec((1, da), fix),
                  pl.BlockSpec((1, da), fix), pl.BlockSpec(sw.shape, fix), pl.BlockSpec((1, 3 * db), fix)],
        out_specs=[pl.BlockSpec((1, L, da), bat), pl.BlockSpec((1, L, db), bat), pl.BlockSpec((1, L, db), bat)],
        out_shape=[jax.ShapeDtypeStruct((B, L, da), BF16), jax.ShapeDtypeStruct((B, L, db), BF16),
                   jax.ShapeDtypeStruct((B, L, db), BF16)],
        scratch_shapes=[pltpu.VMEM((L + 2 * CONV_PAD, da), F32), pltpu.VMEM((L + 2 * CONV_PAD, 3 * db), F32),
                        pltpu.VMEM((8, CONV_ROWS + 2 * CONV_PAD - 8, 128), F32), pltpu.VMEM((CONV_ROWS, da), F32)],
        compiler_params=_cparams(("parallel",)),
    )(a, hy, cw, cb, lg, lb, sw, sb)


def _lconv_kernel(v_ref, x0_ref, wfr_ref, wfi_ref, cw_ref, sw_ref, kare_ref, kaim_ref, kbre_ref, kbim_ref,
                  wir_ref, wii_ref, skip_ref, u_ref, xe_ref, xo_ref, ye_ref, yo_ref, *, nfb, fb):
    j = pl.program_id(1)

    @pl.when(j == 0)
    def _():
        xe_ref[...], xo_ref[...] = _split_rows(v_ref[0])
        ye_ref[...] = jnp.zeros(ye_ref.shape, F32)
        yo_ref[...] = jnp.zeros(yo_ref.shape, F32)

    row0 = _first_row(fb, j)
    wfr, wfi, wir, wii = wfr_ref[...], wfi_ref[...], wir_ref[...], wii_ref[...]
    G = min(LCONV_LANES, v_ref.shape[2])
    for g in range(v_ref.shape[2] // G):
        cols = slice(g * G, (g + 1) * G)
        cw = cw_ref[:, cols]
        sw = sw_ref[:, cols]
        are, aim, bre, bim = _half_spectra(xe_ref[:, cols], xo_ref[:, cols], wfr, wfi, cw, sw, row0)
        kare, kaim, kbre, kbim = kare_ref[:, cols], kaim_ref[:, cols], kbre_ref[:, cols], kbim_ref[:, cols]
        a, b, c, d = are * kare, aim * kaim, bre * kbre, bim * kbim
        pare = a - jnp.where(row0, 0.0, b)
        paim = jnp.where(row0, b - d, are * kaim + aim * kare)
        pbre = c - jnp.where(row0, 0.0, d)
        pbim = jnp.where(row0, aim * kbim + bim * kaim, bre * kbim + bim * kbre)
        q0re = pare + pbre
        q0im = jnp.where(row0, 2.0 * paim, paim - pbim)
        dre = pare - pbre
        dim = paim + pbim
        q1re = dre * cw - dim * sw
        q1im = jnp.where(row0, -2.0 * pbim, dre * sw + dim * cw)
        ye_ref[:, cols] += _dot(wir, q0re.astype(BF16)) + _dot(wii, q0im.astype(BF16))
        yo_ref[:, cols] += _dot(wir, q1re.astype(BF16)) + _dot(wii, q1im.astype(BF16))

    @pl.when(j == nfb - 1)
    def _():
        skip = skip_ref[...]
        x0e, x0o = _split_rows(x0_ref[0])
        oe = x0e.astype(F32) * (0.5 * ye_ref[...] + skip * xe_ref[...].astype(F32))
        oo = x0o.astype(F32) * (0.5 * yo_ref[...] + skip * xo_ref[...].astype(F32))
        u_ref[0] = _merge_rows(oe, oo)


def _long_conv(vv, x0, wf, wi, cw, sw, kspec, skip):
    B, L, db = vv.shape
    L2 = L // 2
    fb = min(LCONV_BINS, L2)
    nfb = L2 // fb
    bat = lambda b, j: (b, 0, 0)
    blk = pl.BlockSpec((fb, db), lambda b, j: (j, 0))
    return pl.pallas_call(
        functools.partial(_lconv_kernel, nfb=nfb, fb=fb),
        grid=(B, nfb),
        in_specs=[pl.BlockSpec((1, L, db), bat), pl.BlockSpec((1, L, db), bat),
                  pl.BlockSpec((fb, L2), lambda b, j: (j, 0)), pl.BlockSpec((fb, L2), lambda b, j: (nfb + j, 0)),
                  blk, blk, blk, blk, blk, blk,
                  pl.BlockSpec((L2, fb), lambda b, j: (0, j)), pl.BlockSpec((L2, fb), lambda b, j: (0, nfb + j)),
                  pl.BlockSpec((1, db), lambda b, j: (0, 0))],
        out_specs=pl.BlockSpec((1, L, db), bat),
        out_shape=jax.ShapeDtypeStruct((B, L, db), BF16),
        scratch_shapes=[pltpu.VMEM((L2, db), BF16), pltpu.VMEM((L2, db), BF16),
                        pltpu.VMEM((L2, db), F32), pltpu.VMEM((L2, db), F32)],
        compiler_params=_cparams(("parallel", "arbitrary")),
    )(vv, x0, wf, wf, cw, sw, *kspec, wi, wi, skip)


def _post_kernel(x_ref, act_ref, u_ref, gate_ref, wa_ref, ba_ref, wb_ref, bb_ref, wo_ref, n2_ref, rw_ref,
                 x1_ref, tok_ref, aff_ref, *, E):
    tm, D = x_ref.shape
    RS = min(POST_ROWS, tm)
    for r0 in range(0, tm, RS):
        rows = slice(r0, r0 + RS)
        ya = _dot(act_ref[rows, :], wa_ref[...]) + ba_ref[...]
        yb = _dot(u_ref[rows, :], wb_ref[...]) + bb_ref[...]
        m = gate_ref[rows, :D].astype(F32) * ya + gate_ref[rows, D:].astype(F32) * yb
        x1 = x_ref[rows, :] + _dot(m.astype(BF16), wo_ref[...])
        x1_ref[rows, :] = x1
        h2 = _rms(x1, n2_ref[...]).astype(BF16)
        tok_ref[rows, :D // 2] = _pack_rows(h2)
        lane = lax.broadcasted_iota(I32, (RS, SIDE_LANES), 1)
        is_expert = lane < E
        logits = jnp.where(is_expert, _dot(h2, rw_ref[...]), -1e30)
        ex = jnp.exp(logits - jnp.max(logits, axis=1, keepdims=True))
        aff = jnp.where(is_expert, ex / jnp.sum(ex, axis=1, keepdims=True), 0.0)
        tok_id = (pl.program_id(0) * tm + r0) % COMBINE_TILE + lax.broadcasted_iota(I32, (RS, SIDE_LANES), 0)
        tok_ref[rows, D // 2:] = jnp.where(lane == E, tok_id, lax.bitcast_convert_type(aff, I32))
        aff_ref[:, rows] = aff.T[0:E, :]


def _post(x, act, u, gate, wa, ba, wb, bb, wo, n2, rw, E):
    T, D = x.shape
    tm = min(TOKEN_TILE, T)
    assert COMBINE_TILE % tm == 0
    RW = D // 2 + SIDE_LANES
    row = lambda i: (i, 0)
    fix = lambda i: (0, 0)
    full = lambda a: pl.BlockSpec(a.shape, fix)
    return pl.pallas_call(
        functools.partial(_post_kernel, E=E),
        grid=(T // tm,),
        in_specs=[pl.BlockSpec((tm, D), row), pl.BlockSpec((tm, act.shape[1]), row),
                  pl.BlockSpec((tm, u.shape[1]), row), pl.BlockSpec((tm, gate.shape[1]), row),
                  full(wa), full(ba), full(wb), full(bb), full(wo), full(n2), full(rw)],
        out_specs=[pl.BlockSpec((tm, D), row), pl.BlockSpec((tm, RW), row), pl.BlockSpec((E, tm), lambda i: (0, i))],
        out_shape=[jax.ShapeDtypeStruct((T, D), F32), jax.ShapeDtypeStruct((T, RW), I32),
                   jax.ShapeDtypeStruct((E, T), F32)],
        compiler_params=_cparams(("parallel",)),
    )(x, act, u, gate, wa, ba, wb, bb, wo, n2, rw)


def _select_kernel(a_ref, idx_ref, ntab_ref, otab_ref, thr_ref, *, cap):
    E, nt, W = a_ref.shape
    nsb = cap // W
    bits = lax.bitcast_convert_type(a_ref[...], I32)

    def bisect(b, cur):
        cand = cur | jnp.left_shift(jnp.int32(1), 30 - b)
        hit = jnp.where(bits >= cand, 1.0, 0.0)
        cnt = jnp.sum(jnp.sum(hit, axis=2, keepdims=True), axis=1, keepdims=True)
        return jnp.where(cnt >= cap, cand, cur)

    thr_ref[...] = lax.fori_loop(0, 31, bisect, jnp.zeros((E, 1, 1), I32))

    r_i = lax.broadcasted_iota(I32, (W, W), 0)
    c_i = lax.broadcasted_iota(I32, (W, W), 1)
    upper = jnp.where(r_i <= c_i, 1.0, 0.0).astype(BF16)
    ones = jnp.ones((W, W), BF16)
    tr = lax.broadcasted_iota(I32, (nt, nt), 0)
    tc = lax.broadcasted_iota(I32, (nt, nt), 1)
    lower = jnp.where(tc < tr, 1.0, 0.0).astype(BF16)
    tile_id = lax.broadcasted_iota(I32, (nt, W), 0).astype(F32)
    lane = lax.broadcasted_iota(I32, (1, W), 1).astype(F32)

    def prefix(maskf):
        mb = maskf.astype(BF16)
        incl = _dot(mb, upper)
        tot = _dot(mb, ones)
        off = _dot(lower, tot.astype(BF16))
        return incl, tot, off

    def expert(e, carry):
        ae = a_ref[e]
        be = lax.bitcast_convert_type(ae, I32)
        th = thr_ref[e]
        gt = be > th
        eq = be == th
        gtf = jnp.where(gt, 1.0, 0.0)
        eqf = jnp.where(eq, 1.0, 0.0)
        need = cap - jnp.sum(jnp.sum(gtf, axis=1, keepdims=True), axis=0, keepdims=True)
        e_incl, _, e_off = prefix(eqf)
        eq_rank = e_off + e_incl - eqf
        sel = gt | (eq & (eq_rank < need))
        self_ = jnp.where(sel, 1.0, 0.0)
        incl, tot, off = prefix(self_)
        ntab_ref[e] = tot[:, 0:128]
        otab_ref[e] = off[:, 0:128]
        incl_t = incl.T.astype(BF16)
        off_hi = off + tot

        def slots(c, carry2):
            s_row = lane + jnp.asarray(c * W, F32)
            oh = jnp.where((off <= s_row) & (s_row < off_hi), 1.0, 0.0)
            tile = jnp.sum(oh * tile_id, axis=0, keepdims=True)
            rank = s_row - jnp.sum(oh * off, axis=0, keepdims=True)
            rt = _dot(incl_t, oh.astype(BF16))
            pos = jnp.sum(jnp.where(rt <= rank, 1.0, 0.0), axis=0, keepdims=True)
            idx_ref[e, pl.ds(c, 1), :] = (tile * W + pos).astype(I32)
            return carry2

        lax.fori_loop(0, nsb, slots, 0)
        return carry

    lax.fori_loop(0, E, expert, 0)


def _select(aff3, cap):
    E, nt, W = aff3.shape
    return pl.pallas_call(
        functools.partial(_select_kernel, cap=cap),
        out_shape=[jax.ShapeDtypeStruct((E, cap // W, W), I32), jax.ShapeDtypeStruct((E, nt, 128), F32),
                   jax.ShapeDtypeStruct((E, nt, 128), F32)],
        scratch_shapes=[pltpu.VMEM((E, 1, 1), I32)],
        compiler_params=pltpu.CompilerParams(vmem_limit_bytes=VMEM_LIMIT),
    )(aff3)


SC_GATHER_ROWS = 128


def _sc_gather(table, idx):
    M = idx.shape[0]
    Dw = table.shape[1]
    win = SC_GATHER_ROWS
    info = plsc.get_sparse_core_info()
    nc = info.num_cores
    nw = nc * info.num_subcores
    per_w = M // nw
    assert M % (nw * win) == 0
    mesh = plsc.VectorSubcoreMesh(core_axis_name="c", subcore_axis_name="s")

    @functools.partial(
        pl.kernel, mesh=mesh,
        out_type=jax.ShapeDtypeStruct((M, Dw), table.dtype),
        scratch_types=[pltpu.VMEM((win,), I32), pltpu.VMEM((win, Dw), table.dtype), pltpu.SemaphoreType.DMA],
    )
    def gather(table_hbm, idx_hbm, out_hbm, idx_v, rows_v, sem):
        wid = lax.axis_index("s") * nc + lax.axis_index("c")

        @pl.loop(0, per_w // win)
        def _(j):
            base = pl.multiple_of(wid * per_w + j * win, win)
            pltpu.sync_copy(idx_hbm.at[pl.ds(base, win)], idx_v)
            pltpu.async_copy(table_hbm.at[idx_v], rows_v, sem).wait()
            pltpu.sync_copy(rows_v, out_hbm.at[pl.ds(base, win)])

    return gather(table, idx)


def _gate_up_kernel(wg_ref, wu_ref, o_ref):
    F = wg_ref.shape[2]
    o_ref[0, :, :F] = wg_ref[0].astype(BF16)
    o_ref[0, :, F:] = wu_ref[0].astype(BF16)


def _gate_up_weights(wg, wu):
    E, D, F = wg.shape
    rb = min(512, D)
    src = pl.BlockSpec((1, rb, F), lambda e, r: (e, r, 0))
    return pl.pallas_call(
        _gate_up_kernel,
        grid=(E, D // rb),
        in_specs=[src, src],
        out_specs=pl.BlockSpec((1, rb, 2 * F), lambda e, r: (e, r, 0)),
        out_shape=jax.ShapeDtypeStruct((E, D, 2 * F), BF16),
        compiler_params=_cparams(("parallel", "parallel")),
    )(wg, wu)


def _ffn_kernel(xs_ref, wgu_ref, wd_ref, ye_ref):
    R = xs_ref.shape[0]
    half = wgu_ref.shape[1] // 2
    F = wd_ref.shape[1]
    xs = _unpack_rows(xs_ref[:, :half])
    side = xs_ref[:, half:]
    lane = lax.broadcasted_iota(I32, (R, SIDE_LANES), 1)
    aff = lax.bitcast_convert_type(side, F32)
    gate = jnp.sum(jnp.where(lane == pl.program_id(0), aff, 0.0), axis=1, keepdims=True)
    gu = _dot(xs, wgu_ref[0])
    g = gu[:, :F]
    hid = (g * jax.nn.sigmoid(g) * gu[:, F:]).astype(BF16)
    ye = _dot(hid, wd_ref[0]) * gate
    ye_ref[:, :half] = _pack_rows(ye.astype(BF16))
    ye_ref[:, half:] = side


def _expert_ffn(xs, wgu, wd):
    E, F, D = wd.shape
    cap = xs.shape[0] // E
    R = min(FFN_ROWS, cap)
    assert cap % R == 0 and wgu.shape == (E, D, 2 * F)
    RW = xs.shape[1]
    nsb = cap // R
    return pl.pallas_call(
        _ffn_kernel,
        grid=(E, nsb),
        in_specs=[pl.BlockSpec((R, RW), lambda e, s: (e * nsb + s, 0)),
                  pl.BlockSpec((1, D, 2 * F), lambda e, s: (e, 0, 0)),
                  pl.BlockSpec((1, F, D), lambda e, s: (e, 0, 0))],
        out_specs=pl.BlockSpec((R, RW), lambda e, s: (e * nsb + s, 0)),
        out_shape=jax.ShapeDtypeStruct(xs.shape, I32),
        compiler_params=_cparams(("parallel", "parallel")),
    )(xs, wgu, wd)


def _combine_tables(ntab, otab, cap):
    E, nt = ntab.shape
    W = COMBINE_ROWS
    g = COMBINE_TILE // ROUTE_TILE
    ntc = nt // g
    n2 = ntab.reshape(E, ntc, g).sum(-1)
    off2 = otab.reshape(E, ntc, g)[:, :, 0]
    hi = jnp.cumsum(n2, axis=0)
    lo = hi - n2
    m = hi[-1]
    nch = jnp.maximum(1, (m + W - 1) // W)
    cb_hi = jnp.cumsum(nch)
    cb_lo = cb_hi - nch
    nc_max = (E * cap) // W + ntc
    c = jnp.arange(nc_max, dtype=I32)
    tile = jnp.minimum(jnp.sum((cb_hi[None, :] <= c[:, None]).astype(I32), axis=1), ntc - 1)
    k = jnp.where(c < cb_hi[-1], c - cb_lo[tile], nch[ntc - 1] + c - cb_hi[-1])
    delta = jnp.arange(E, dtype=I32)[:, None] * cap + off2 - lo
    bcast = lambda a: jnp.broadcast_to(a.T.astype(F32)[:, :, None], (ntc, E, 128))
    blk = jnp.minimum(c, cb_hi[-1] - 1)
    sched = (tile, k.astype(I32), m.astype(I32), nch.astype(I32), blk.astype(I32))
    return sched, bcast(lo), bcast(hi), bcast(delta)


def _plan_kernel(tile_s, k_s, lo_ref, hi_ref, dl_ref, src_ref, *, n_rows):
    W = src_ref.shape[1]
    lane = lax.broadcasted_iota(I32, (1, W), 1).astype(F32)

    def chunk(c, carry):
        t = tile_s[c]
        q = lane + jnp.asarray(k_s[c] * W, F32)
        lo = lo_ref[t][:, 0:1]
        hi = hi_ref[t][:, 0:1]
        mine = (lo <= q) & (q < hi)
        src = jnp.sum(jnp.where(mine, dl_ref[t][:, 0:1] + q, 0.0), axis=0, keepdims=True)
        used = jnp.sum(jnp.where(mine, 1.0, 0.0), axis=0, keepdims=True) > 0.0
        p = lane + jnp.asarray(c * W, F32)
        spare = jnp.where(p < n_rows, p, p - n_rows)
        src_ref[pl.ds(c, 1), :] = jnp.where(used, src, spare).astype(I32)
        return carry

    lax.fori_loop(0, src_ref.shape[0], chunk, 0)


def _combine_plan(tile, k, lo, hi, delta, n_rows):
    smem = pl.BlockSpec(memory_space=pltpu.SMEM)
    vmem = pl.BlockSpec(memory_space=pltpu.VMEM)
    assert tile.shape[0] * COMBINE_ROWS <= 2 * n_rows
    return pl.pallas_call(
        functools.partial(_plan_kernel, n_rows=n_rows),
        in_specs=[smem, smem, vmem, vmem, vmem],
        out_specs=vmem,
        out_shape=jax.ShapeDtypeStruct((tile.shape[0], COMBINE_ROWS), I32),
        compiler_params=pltpu.CompilerParams(vmem_limit_bytes=VMEM_LIMIT),
    )(tile, k, lo, hi, delta)


def _combine_kernel(tile_s, k_s, m_s, nch_s, blk_s, x1_ref, rows_ref, g_ref, y_ref, acc_ref, *, E):
    c = pl.program_id(0)
    t = tile_s[c]
    k = k_s[c]
    W = rows_ref.shape[0]
    CT, D = x1_ref.shape

    @pl.when(k == 0)
    def _():
        acc_ref[...] = x1_ref[...]

    @pl.when(k < nch_s[t])
    def _():
        rows = _unpack_rows(rows_ref[:, :D // 2])
        side_t = rows_ref[:, D // 2:].astype(F32).T
        tok = side_t[E:E + 1, :]
        q = k * W + lax.broadcasted_iota(I32, (1, W), 1)
        hit = (tok == lax.broadcasted_iota(I32, (CT, W), 0).astype(F32)) & (q < m_s[t])
        onehot = jnp.where(hit, 1.0, 0.0).astype(BF16)
        acc_ref[...] += _dot(onehot, rows)

    @pl.when(k == nch_s[t] - 1)
    def _():
        y_ref[...] = _rms(acc_ref[...], g_ref[...])


def _combine(sched, x1, rows, g, E):
    T, D = x1.shape
    W = COMBINE_ROWS
    CT = COMBINE_TILE
    RW = rows.shape[1]
    grid_spec = pltpu.PrefetchScalarGridSpec(
        num_scalar_prefetch=len(sched),
        grid=(sched[0].shape[0],),
        in_specs=[pl.BlockSpec((CT, D), lambda c, tile_s, *_: (tile_s[c], 0)),
                  pl.BlockSpec((W, RW), lambda c, tile_s, k_s, m_s, nch_s, blk_s: (blk_s[c], 0)),
                  pl.BlockSpec((1, D), lambda c, *_: (0, 0))],
        out_specs=pl.BlockSpec((CT, D), lambda c, tile_s, *_: (tile_s[c], 0)),
        scratch_shapes=[pltpu.VMEM((CT, D), F32)],
    )
    return pl.pallas_call(
        functools.partial(_combine_kernel, E=E),
        grid_spec=grid_spec,
        out_shape=jax.ShapeDtypeStruct((T, D), F32),
        compiler_params=_cparams(("arbitrary",)),
    )(*sched, x1, rows, g)


def _layer(x, p, spec_cache):
    B, L, D = x.shape
    T = B * L
    da = p["conv_dw_w"].shape[1]
    db = p["hy_skip"].shape[1]
    E = p["n_experts"]
    W = ROUTE_TILE
    cap = max(1, min(T, CAPACITY_FACTOR * T // E))
    assert T % COMBINE_TILE == 0 and cap % W == 0 and L % CONV_ROWS == 0 and E < SIDE_LANES

    if L not in spec_cache:
        wf, wi = _dft_tables(L // 2)
        cw, sw = _twiddles(L, db)
        kspec = _filter_spectrum(L, p["hy_f_w1"], p["hy_f_b1"], p["hy_f_w2"], p["hy_f_b2"], p["hy_f_w3"],
                                 p["hy_f_b3"], p["hy_f_w4"], p["hy_f_freq"], wf, cw, sw)
        spec_cache[L] = (wf, wi, cw, sw, kspec)
    wf, wi, cw, sw, kspec = spec_cache[L]

    x2 = x.reshape(T, D)
    a, hy, gate = _inproj(x2, p["norm1_g"], p["w_in"], p["b_in"], da, db)
    act, x0, vv = _convs(a.reshape(B, L, da), hy.reshape(B, L, 3 * db), p["conv_dw_w"], p["conv_dw_b"],
                         p["conv_ln_g"], p["conv_ln_b"], p["hy_short_w"], p["hy_short_b"])
    u = _long_conv(vv, x0, wf, wi, cw, sw, kspec, p["hy_skip"])
    x1, tok, aff = _post(x2, act.reshape(T, da), u.reshape(T, db), gate, p["conv_w_out"], p["conv_b_out"],
                         p["hy_w_out"], p["hy_b_out"], p["w_o"], p["norm2_g"], p["router_w"], E)
    nt = T // W
    idx, ntab, otab = _select(aff.reshape(E, nt, W), cap)
    xs = _sc_gather(tok, idx.reshape(E * cap))
    ye = _expert_ffn(xs, p["exp_w_gate_up"], p["exp_w_down"])
    sched, lo, hi, delta = _combine_tables(ntab[:, :, 0].astype(I32), otab[:, :, 0].astype(I32), cap)
    src = _combine_plan(sched[0], sched[1], lo, hi, delta, E * cap)
    rows = _sc_gather(ye, src.reshape(-1))
    y = _combine(sched, x1, rows, p["norm_f_g"], E)
    return y.reshape(B, L, D)


def kernel(x_prompt, x_sample, norm1_g, w_in, b_in, conv_dw_w, conv_dw_b, conv_ln_g, conv_ln_b, conv_w_out, conv_b_out, hy_short_w, hy_short_b, hy_f_w1, hy_f_b1, hy_f_w2, hy_f_b2, hy_f_w3, hy_f_b3, hy_f_w4, hy_f_freq, hy_skip, hy_w_out, hy_b_out, w_o, norm2_g, router_w, exp_w_gate, exp_w_up, exp_w_down, norm_f_g):
    depth = w_in.shape[0]
    assert depth == 1
    i = 0
    row = lambda a: a[i][None].astype(F32)
    p = {
        "norm1_g": row(norm1_g), "w_in": w_in[i].astype(BF16), "b_in": row(b_in),
        "conv_dw_w": conv_dw_w[i].astype(F32), "conv_dw_b": row(conv_dw_b),
        "conv_ln_g": row(conv_ln_g), "conv_ln_b": row(conv_ln_b),
        "conv_w_out": conv_w_out[i].astype(BF16), "conv_b_out": row(conv_b_out),
        "hy_short_w": hy_short_w[i].astype(F32), "hy_short_b": row(hy_short_b),
        "hy_f_w1": hy_f_w1[i], "hy_f_b1": hy_f_b1[i], "hy_f_w2": hy_f_w2[i], "hy_f_b2": hy_f_b2[i],
        "hy_f_w3": hy_f_w3[i], "hy_f_b3": hy_f_b3[i], "hy_f_w4": hy_f_w4[i], "hy_f_freq": hy_f_freq[i],
        "hy_skip": row(hy_skip), "hy_w_out": hy_w_out[i].astype(BF16), "hy_b_out": row(hy_b_out),
        "w_o": w_o[i].astype(BF16), "norm2_g": row(norm2_g), "n_experts": router_w.shape[2],
        "router_w": jnp.pad(router_w[i], ((0, 0), (0, SIDE_LANES - router_w.shape[2]))).astype(BF16),
        "exp_w_gate_up": _gate_up_weights(exp_w_gate[i], exp_w_up[i]),
        "exp_w_down": exp_w_down[i].astype(BF16), "norm_f_g": norm_f_g[None].astype(F32),
    }
    spec_cache = {}
    y_sample = _layer(x_sample, p, spec_cache)
    y_prompt = _layer(x_prompt, p, spec_cache)
    return (y_prompt, y_sample)
```

```python
import functools
import math

import jax
import jax.numpy as jnp
from jax import lax
from jax.experimental import pallas as pl
from jax.experimental.pallas import tpu as pltpu
from jax.experimental.pallas import tpu_sc as plsc

F32 = jnp.float32
BF16 = jnp.bfloat16
I32 = jnp.int32

NORM_EPS = 1e-6
LN_EPS = 1e-5
FAST_DECAY_PCT = 0.3
SLOW_DECAY_PCT = 1.5
DECAY_TARGET = 1e-2
CAPACITY_FACTOR = 2

ROUTE_TILE = 256
TOKEN_TILE = 512
CONV_ROWS = 128
CONV_PAD = 16
POST_ROWS = 256
LCONV_BINS = 512
LCONV_LANES = 256
FFN_ROWS = 1024
COMBINE_TILE = 512
COMBINE_ROWS = 512
SIDE_LANES = 128
VMEM_LIMIT = 56 * 1024 * 1024


def _cparams(sem):
    return pltpu.CompilerParams(dimension_semantics=sem, vmem_limit_bytes=VMEM_LIMIT)


def _dot(a, b):
    return jnp.dot(a, b, preferred_element_type=F32)


def _dot_nt(a, b, precision=None):
    return lax.dot_general(a, b, (((1,), (1,)), ((), ())), preferred_element_type=F32, precision=precision)


def _rms(x, g):
    return x * lax.rsqrt(jnp.mean(x * x, axis=-1, keepdims=True) + NORM_EPS) * g


_HI_MASK = -65536


def _pack_rows(xb):
    half = xb.shape[1] // 2
    lo = lax.bitcast_convert_type(xb[:, :half].astype(F32), I32)
    hi = lax.bitcast_convert_type(xb[:, half:].astype(F32), I32)
    return (hi & _HI_MASK) | lax.shift_right_logical(lo, jnp.full(lo.shape, 16, I32))


def _unpack_rows(w):
    lo = lax.bitcast_convert_type(lax.shift_left(w, jnp.full(w.shape, 16, I32)), F32)
    hi = lax.bitcast_convert_type(w & _HI_MASK, F32)
    return jnp.concatenate([lo, hi], axis=1).astype(BF16)


def _filter_kernel(z_ref, w1_ref, b1_ref, w2_ref, b2_ref, w3_ref, b3_ref, w4_ref, fr_ref, dl_ref, sd_ref):
    hp = lax.Precision.HIGHEST
    fr = fr_ref[...]
    z = z_ref[...]
    h = jnp.sin(fr * (jnp.dot(z, w1_ref[...], precision=hp, preferred_element_type=F32) + b1_ref[...]))
    h = jnp.sin(fr * (jnp.dot(h, w2_ref[...], precision=hp, preferred_element_type=F32) + b2_ref[...]))
    h = jnp.sin(fr * (jnp.dot(h, w3_ref[...], precision=hp, preferred_element_type=F32) + b3_ref[...]))
    h4 = jnp.dot(h, w4_ref[...], precision=hp, preferred_element_type=F32)
    L = z.shape[0]
    db = dl_ref.shape[1]
    t = z[:, 0:1]
    decay = jnp.exp(-t * dl_ref[...])
    hf = h4[:, :db] * decay
    hb = h4[:, db:] * decay
    norm = jnp.sum(jnp.abs(hf), axis=0, keepdims=True) + jnp.sum(jnp.abs(hb), axis=0, keepdims=True)
    hf = hf / norm
    hb = hb / norm
    row = lax.broadcasted_iota(I32, (L, 1), 0)
    hbs = jnp.where(row == 0, 0.0, pltpu.roll(hb, 1, axis=0))
    s = hf + hbs
    d = hf - hbs
    s_hi = s.astype(BF16)
    d_hi = d.astype(BF16)
    sd_ref[:, 0 * db:1 * db] = s_hi
    sd_ref[:, 1 * db:2 * db] = (s - s_hi.astype(F32)).astype(BF16)
    sd_ref[:, 2 * db:3 * db] = d_hi
    sd_ref[:, 3 * db:4 * db] = (d - d_hi.astype(F32)).astype(BF16)


def _split_rows(x):
    w = pltpu.bitcast(x, I32)
    even = lax.bitcast_convert_type(lax.shift_left(w, jnp.full(w.shape, 16, I32)), F32).astype(BF16)
    odd = lax.bitcast_convert_type(w & _HI_MASK, F32).astype(BF16)
    return even, odd


def _merge_rows(even, odd):
    e = lax.bitcast_convert_type(even.astype(BF16).astype(F32), I32)
    o = lax.bitcast_convert_type(odd.astype(BF16).astype(F32), I32)
    return pltpu.bitcast((o & _HI_MASK) | lax.shift_right_logical(e, jnp.full(e.shape, 16, I32)), BF16)


def _half_spectra(xe, xo, wfr, wfi, cw, sw, row0):
    ere = _dot(wfr, xe)
    eim = _dot(wfi, xe)
    ore = _dot(wfr, xo)
    oim = _dot(wfi, xo)
    tre = cw * ore + sw * oim
    tim = cw * oim - sw * ore
    return ere + tre, jnp.where(row0, eim, eim + tim), ere - tre, jnp.where(row0, -oim, tim - eim)


def _first_row(fb, block):
    return (lax.broadcasted_iota(I32, (fb, 1), 0) + block * fb) == 0


def _spec_kernel(sd_ref, wfr_ref, wfi_ref, cw_ref, sw_ref, kare_ref, kaim_ref, kbre_ref, kbim_ref):
    fb, db = kare_ref.shape
    row0 = _first_row(fb, pl.program_id(0))
    xe, xo = _split_rows(sd_ref[...])

    def spectra(g):
        cols = slice(g * db, (g + 1) * db)
        return _half_spectra(xe[:, cols], xo[:, cols], wfr_ref[...], wfi_ref[...], cw_ref[...], sw_ref[...], row0)

    s_hi, s_lo, d_hi, d_lo = spectra(0), spectra(1), spectra(2), spectra(3)
    kare_ref[...] = s_hi[0] + s_lo[0]
    kbre_ref[...] = s_hi[2] + s_lo[2]
    kaim_ref[...] = jnp.where(row0, s_hi[1] + s_lo[1], d_hi[1] + d_lo[1])
    kbim_ref[...] = d_hi[3] + d_lo[3]


def _dft_tables(L):
    n2 = 2 * L
    k = jnp.arange(L, dtype=I32)[:, None]
    n = jnp.arange(L, dtype=I32)[None, :]
    ph = ((k * n) % n2).astype(F32) * (2.0 * math.pi / n2)
    c = jnp.cos(ph)
    s = jnp.sin(ph)
    alt = jnp.where(n % 2 == 0, 1.0, -1.0).astype(F32)
    wf_im = jnp.where(k == 0, alt, -s)
    wf = jnp.concatenate([c, wf_im], axis=0).astype(BF16)
    ck = jnp.where(k == 0, 1.0, 2.0) / n2
    wi_re = (c * ck).T
    wi_im = jnp.where(k == 0, alt / n2, -s * ck).T
    wi = jnp.concatenate([wi_re, wi_im], axis=1).astype(BF16)
    return wf, wi


def _twiddles(L, db):
    ph = jnp.arange(L // 2, dtype=F32)[:, None] * (math.pi / L)
    return jnp.broadcast_to(jnp.cos(ph), (L // 2, db)), jnp.broadcast_to(jnp.sin(ph), (L // 2, db))


def _filter_spectrum(L, w1, b1, w2, b2, w3, b3, w4, freq, wf, cw, sw):
    pe, ffn = w1.shape
    db = w4.shape[1] // 2
    bands = (pe - 1) // 2
    t = jnp.linspace(0.0, 1.0, L, dtype=F32)[:, None]
    w = 2.0 * math.pi * jnp.arange(L, dtype=F32)[:, None] / L
    bnd = jnp.linspace(1e-4, bands - 1, bands, dtype=F32)[None, :]
    z = jnp.concatenate([t, jnp.cos(bnd * w), -jnp.sin(bnd * w)], axis=-1)
    P = 128
    zp = jnp.zeros((L, P), F32).at[:, :pe].set(z)
    pad2 = lambda a, r, c: jnp.zeros((r, c), F32).at[:a.shape[0], :a.shape[1]].set(a.astype(F32))
    max_decay = math.log(DECAY_TARGET) / FAST_DECAY_PCT
    min_decay = math.log(DECAY_TARGET) / SLOW_DECAY_PCT
    deltas = jnp.abs(jnp.linspace(min_decay, max_decay, db, dtype=F32))[None, :]
    sd = pl.pallas_call(
        _filter_kernel,
        out_shape=jax.ShapeDtypeStruct((L, 4 * db), BF16),
        compiler_params=pltpu.CompilerParams(vmem_limit_bytes=VMEM_LIMIT),
    )(zp, pad2(w1, P, P), pad2(b1[None], 1, P), pad2(w2, P, P), pad2(b2[None], 1, P), pad2(w3, P, P),
      pad2(b3[None], 1, P), pad2(w4, P, 2 * db), pad2(freq[None], 1, P), deltas)
    L2 = L // 2
    fb = min(LCONV_BINS, L2)
    nfb = L2 // fb
    blk = pl.BlockSpec((fb, db), lambda j: (j, 0))
    return pl.pallas_call(
        _spec_kernel,
        grid=(nfb,),
        in_specs=[pl.BlockSpec((L, 4 * db), lambda j: (0, 0)),
                  pl.BlockSpec((fb, L2), lambda j: (j, 0)), pl.BlockSpec((fb, L2), lambda j: (nfb + j, 0)), blk, blk],
        out_specs=[blk] * 4,
        out_shape=[jax.ShapeDtypeStruct((L2, db), F32)] * 4,
        compiler_params=_cparams(("parallel",)),
    )(sd, wf, wf, cw, sw)


def _inproj_kernel(x_ref, g_ref, w_ref, b_ref, a_ref, hy_ref, gate_ref, *, da, db):
    h = _rms(x_ref[...], g_ref[...]).astype(BF16)
    c0, c1, c2 = da, 2 * da, 2 * da + 3 * db
    za = _dot(h, w_ref[:, 0:c0]) + b_ref[:, 0:c0]
    zg = _dot(h, w_ref[:, c0:c1]) + b_ref[:, c0:c1]
    a_ref[...] = (za * jax.nn.sigmoid(zg)).astype(BF16)
    hy_ref[...] = (_dot(h, w_ref[:, c1:c2]) + b_ref[:, c1:c2]).astype(BF16)
    gate_ref[...] = jax.nn.sigmoid(_dot(h, w_ref[:, c2:]) + b_ref[:, c2:]).astype(BF16)


def _inproj(x, g, w, b, da, db):
    T, D = x.shape
    C = w.shape[1]
    tm = min(TOKEN_TILE, T)
    ng = C - 2 * da - 3 * db
    row = lambda i: (i, 0)
    fix = lambda i: (0, 0)
    return pl.pallas_call(
        functools.partial(_inproj_kernel, da=da, db=db),
        grid=(T // tm,),
        in_specs=[pl.BlockSpec((tm, D), row), pl.BlockSpec((1, D), fix), pl.BlockSpec((D, C), fix),
                  pl.BlockSpec((1, C), fix)],
        out_specs=[pl.BlockSpec((tm, da), row), pl.BlockSpec((tm, 3 * db), row), pl.BlockSpec((tm, ng), row)],
        out_shape=[jax.ShapeDtypeStruct((T, da), BF16), jax.ShapeDtypeStruct((T, 3 * db), BF16),
                   jax.ShapeDtypeStruct((T, ng), BF16)],
        compiler_params=_cparams(("parallel",)),
    )(x, g, w, b)


def _conv_kernel(a_ref, hy_ref, cw_ref, cb_ref, lg_ref, lb_ref, sw_ref, sb_ref,
                 act_ref, x0_ref, vv_ref, pa_ref, ph_ref, sh_ref, cv_ref, *, L, da, db):
    K = cw_ref.shape[0]
    KS = sw_ref.shape[0]
    R = CONV_ROWS
    P = CONV_PAD
    LANES = 128
    zeros_a = jnp.zeros((P, da), F32)
    zeros_h = jnp.zeros((P, 3 * db), F32)
    pa_ref[0:P, :] = zeros_a
    pa_ref[P + L:P + L + P, :] = zeros_a
    ph_ref[0:P, :] = zeros_h
    ph_ref[P + L:P + L + P, :] = zeros_h
    pa_ref[P:P + L, :] = a_ref[0].astype(F32)
    ph_ref[P:P + L, :] = hy_ref[0].astype(F32)

    def chunk(c, carry):
        base = pl.multiple_of(c * R, R)
        for cb in range(da // LANES):
            cols = slice(cb * LANES, (cb + 1) * LANES)
            win = pa_ref[pl.ds(base, R + 2 * P), cols]
            for r in range(8):
                sh_ref[r] = win[r:r + R + 2 * P - 8, :]
            acc = jnp.zeros((R, LANES), F32) + cb_ref[:, cols]
            for k in range(K):
                q, r = divmod(P - K // 2 + k, 8)
                acc = acc + cw_ref[k:k + 1, cols] * sh_ref[r, 8 * q:8 * q + R, :]
            cv_ref[:, cols] = acc
        acc = cv_ref[...]
        mu = jnp.mean(acc, axis=-1, keepdims=True)
        xc = acc - mu
        var = jnp.mean(xc * xc, axis=-1, keepdims=True)
        y = xc * lax.rsqrt(var + LN_EPS) * lg_ref[...] + lb_ref[...]
        act_ref[0, pl.ds(base, R), :] = (y * jax.nn.sigmoid(y)).astype(BF16)

        def short(lo):
            cols = slice(lo, lo + LANES)
            win = ph_ref[pl.ds(base, R + 2 * P), cols]
            u = jnp.zeros((R, LANES), F32) + sb_ref[:, cols]
            for k in range(KS):
                o = P - KS // 2 + k
                u = u + sw_ref[k:k + 1, cols] * win[o:o + R, :]
            return u

        for cb in range(db // LANES):
            lo = cb * LANES
            cols = slice(lo, lo + LANES)
            vv_ref[0, pl.ds(base, R), cols] = (short(db + lo) * short(2 * db + lo)).astype(BF16)
            x0_ref[0, pl.ds(base, R), cols] = short(lo).astype(BF16)
        return carry

    lax.fori_loop(0, L // R, chunk, 0)


def _convs(a, hy, cw, cb, lg, lb, sw, sb):
    B, L, da = a.shape
    db = hy.shape[2] // 3
    bat = lambda b: (b, 0, 0)
    fix = lambda b: (0, 0)
    return pl.pallas_call(
        functools.partial(_conv_kernel, L=L, da=da, db=db),
        grid=(B,),
        in_specs=[pl.BlockSpec((1, L, da), bat), pl.BlockSpec((1, L, 3 * db), bat),
                  pl.BlockSpec(cw.shape, fix), pl.BlockSpec((1, da), fix), pl.BlockSpec((1, da), fix),
                  pl.BlockSpec((1, da), fix), pl.BlockSpec(sw.shape, fix), pl.BlockSpec((1, 3 * db), fix)],
        out_specs=[pl.BlockSpec((1, L, da), bat), pl.BlockSpec((1, L, db), bat), pl.BlockSpec((1, L, db), bat)],
        out_shape=[jax.ShapeDtypeStruct((B, L, da), BF16), jax.ShapeDtypeStruct((B, L, db), BF16),
                   jax.ShapeDtypeStruct((B, L, db), BF16)],
        scratch_shapes=[pltpu.VMEM((L + 2 * CONV_PAD, da), F32), pltpu.VMEM((L + 2 * CONV_PAD, 3 * db), F32),
                        pltpu.VMEM((8, CONV_ROWS + 2 * CONV_PAD - 8, 128), F32), pltpu.VMEM((CONV_ROWS, da), F32)],
        compiler_params=_cparams(("parallel",)),
    )(a, hy, cw, cb, lg, lb, sw, sb)


def _lconv_kernel(v_ref, x0_ref, wfr_ref, wfi_ref, cw_ref, sw_ref, kare_ref, kaim_ref, kbre_ref, kbim_ref,
                  wir_ref, wii_ref, skip_ref, u_ref, xe_ref, xo_ref, ye_ref, yo_ref, *, nfb, fb):
    j = pl.program_id(1)

    @pl.when(j == 0)
    def _():
        xe_ref[...], xo_ref[...] = _split_rows(v_ref[0])
        ye_ref[...] = jnp.zeros(ye_ref.shape, F32)
        yo_ref[...] = jnp.zeros(yo_ref.shape, F32)

    row0 = _first_row(fb, j)
    wfr, wfi, wir, wii = wfr_ref[...], wfi_ref[...], wir_ref[...], wii_ref[...]
    G = min(LCONV_LANES, v_ref.shape[2])
    for g in range(v_ref.shape[2] // G):
        cols = slice(g * G, (g + 1) * G)
        cw = cw_ref[:, cols]
        sw = sw_ref[:, cols]
        are, aim, bre, bim = _half_spectra(xe_ref[:, cols], xo_ref[:, cols], wfr, wfi, cw, sw, row0)
        kare, kaim, kbre, kbim = kare_ref[:, cols], kaim_ref[:, cols], kbre_ref[:, cols], kbim_ref[:, cols]
        a, b, c, d = are * kare, aim * kaim, bre * kbre, bim * kbim
        pare = a - jnp.where(row0, 0.0, b)
        paim = jnp.where(row0, b - d, are * kaim + aim * kare)
        pbre = c - jnp.where(row0, 0.0, d)
        pbim = jnp.where(row0, aim * kbim + bim * kaim, bre * kbim + bim * kbre)
        q0re = pare + pbre
        q0im = jnp.where(row0, 2.0 * paim, paim - pbim)
        dre = pare - pbre
        dim = paim + pbim
        q1re = dre * cw - dim * sw
        q1im = jnp.where(row0, -2.0 * pbim, dre * sw + dim * cw)
        ye_ref[:, cols] += _dot(wir, q0re.astype(BF16)) + _dot(wii, q0im.astype(BF16))
        yo_ref[:, cols] += _dot(wir, q1re.astype(BF16)) + _dot(wii, q1im.astype(BF16))

    @pl.when(j == nfb - 1)
    def _():
        skip = skip_ref[...]
        x0e, x0o = _split_rows(x0_ref[0])
        oe = x0e.astype(F32) * (0.5 * ye_ref[...] + skip * xe_ref[...].astype(F32))
        oo = x0o.astype(F32) * (0.5 * yo_ref[...] + skip * xo_ref[...].astype(F32))
        u_ref[0] = _merge_rows(oe, oo)


def _long_conv(vv, x0, wf, wi, cw, sw, kspec, skip):
    B, L, db = vv.shape
    L2 = L // 2
    fb = min(LCONV_BINS, L2)
    nfb = L2 // fb
    bat = lambda b, j: (b, 0, 0)
    blk = pl.BlockSpec((fb, db), lambda b, j: (j, 0))
    return pl.pallas_call(
        functools.partial(_lconv_kernel, nfb=nfb, fb=fb),
        grid=(B, nfb),
        in_specs=[pl.BlockSpec((1, L, db), bat), pl.BlockSpec((1, L, db), bat),
                  pl.BlockSpec((fb, L2), lambda b, j: (j, 0)), pl.BlockSpec((fb, L2), lambda b, j: (nfb + j, 0)),
                  blk, blk, blk, blk, blk, blk,
                  pl.BlockSpec((L2, fb), lambda b, j: (0, j)), pl.BlockSpec((L2, fb), lambda b, j: (0, nfb + j)),
                  pl.BlockSpec((1, db), lambda b, j: (0, 0))],
        out_specs=pl.BlockSpec((1, L, db), bat),
        out_shape=jax.ShapeDtypeStruct((B, L, db), BF16),
        scratch_shapes=[pltpu.VMEM((L2, db), BF16), pltpu.VMEM((L2, db), BF16),
                        pltpu.VMEM((L2, db), F32), pltpu.VMEM((L2, db), F32)],
        compiler_params=_cparams(("parallel", "arbitrary")),
    )(vv, x0, wf, wf, cw, sw, *kspec, wi, wi, skip)


def _post_kernel(x_ref, act_ref, u_ref, gate_ref, wa_ref, ba_ref, wb_ref, bb_ref, wo_ref, n2_ref, rw_ref,
                 x1_ref, tok_ref, aff_ref, *, E):
    tm, D = x_ref.shape
    RS = min(POST_ROWS, tm)
    for r0 in range(0, tm, RS):
        rows = slice(r0, r0 + RS)
        ya = _dot(act_ref[rows, :], wa_ref[...]) + ba_ref[...]
        yb = _dot(u_ref[rows, :], wb_ref[...]) + bb_ref[...]
        m = gate_ref[rows, :D].astype(F32) * ya + gate_ref[rows, D:].astype(F32) * yb
        x1 = x_ref[rows, :] + _dot(m.astype(BF16), wo_ref[...])
        x1_ref[rows, :] = x1
        h2 = _rms(x1, n2_ref[...]).astype(BF16)
        tok_ref[rows, :D // 2] = _pack_rows(h2)
        lane = lax.broadcasted_iota(I32, (RS, SIDE_LANES), 1)
        is_expert = lane < E
        logits = jnp.where(is_expert, _dot(h2, rw_ref[...]), -1e30)
        ex = jnp.exp(logits - jnp.max(logits, axis=1, keepdims=True))
        aff = jnp.where(is_expert, ex / jnp.sum(ex, axis=1, keepdims=True), 0.0)
        tok_id = (pl.program_id(0) * tm + r0) % COMBINE_TILE + lax.broadcasted_iota(I32, (RS, SIDE_LANES), 0)
        tok_ref[rows, D // 2:] = jnp.where(lane == E, tok_id, lax.bitcast_convert_type(aff, I32))
        aff_ref[:, rows] = aff.T[0:E, :]


def _post(x, act, u, gate, wa, ba, wb, bb, wo, n2, rw, E):
    T, D = x.shape
    tm = min(TOKEN_TILE, T)
    assert COMBINE_TILE % tm == 0
    RW = D // 2 + SIDE_LANES
    row = lambda i: (i, 0)
    fix = lambda i: (0, 0)
    full = lambda a: pl.BlockSpec(a.shape, fix)
    return pl.pallas_call(
        functools.partial(_post_kernel, E=E),
        grid=(T // tm,),
        in_specs=[pl.BlockSpec((tm, D), row), pl.BlockSpec((tm, act.shape[1]), row),
                  pl.BlockSpec((tm, u.shape[1]), row), pl.BlockSpec((tm, gate.shape[1]), row),
                  full(wa), full(ba), full(wb), full(bb), full(wo), full(n2), full(rw)],
        out_specs=[pl.BlockSpec((tm, D), row), pl.BlockSpec((tm, RW), row), pl.BlockSpec((E, tm), lambda i: (0, i))],
        out_shape=[jax.ShapeDtypeStruct((T, D), F32), jax.ShapeDtypeStruct((T, RW), I32),
                   jax.ShapeDtypeStruct((E, T), F32)],
        compiler_params=_cparams(("parallel",)),
    )(x, act, u, gate, wa, ba, wb, bb, wo, n2, rw)


def _select_kernel(a_ref, idx_ref, ntab_ref, otab_ref, thr_ref, *, cap):
    E, nt, W = a_ref.shape
    nsb = cap // W
    bits = lax.bitcast_convert_type(a_ref[...], I32)

    def bisect(b, cur):
        cand = cur | jnp.left_shift(jnp.int32(1), 30 - b)
        hit = jnp.where(bits >= cand, 1.0, 0.0)
        cnt = jnp.sum(jnp.sum(hit, axis=2, keepdims=True), axis=1, keepdims=True)
        return jnp.where(cnt >= cap, cand, cur)

    thr_ref[...] = lax.fori_loop(0, 31, bisect, jnp.zeros((E, 1, 1), I32))

    r_i = lax.broadcasted_iota(I32, (W, W), 0)
    c_i = lax.broadcasted_iota(I32, (W, W), 1)
    upper = jnp.where(r_i <= c_i, 1.0, 0.0).astype(BF16)
    ones = jnp.ones((W, W), BF16)
    tr = lax.broadcasted_iota(I32, (nt, nt), 0)
    tc = lax.broadcasted_iota(I32, (nt, nt), 1)
    lower = jnp.where(tc < tr, 1.0, 0.0).astype(BF16)
    tile_id = lax.broadcasted_iota(I32, (nt, W), 0).astype(F32)
    lane = lax.broadcasted_iota(I32, (1, W), 1).astype(F32)

    def prefix(maskf):
        mb = maskf.astype(BF16)
        incl = _dot(mb, upper)
        tot = _dot(mb, ones)
        off = _dot(lower, tot.astype(BF16))
        return incl, tot, off

    def expert(e, carry):
        ae = a_ref[e]
        be = lax.bitcast_convert_type(ae, I32)
        th = thr_ref[e]
        gt = be > th
        eq = be == th
        gtf = jnp.where(gt, 1.0, 0.0)
        eqf = jnp.where(eq, 1.0, 0.0)
        need = cap - jnp.sum(jnp.sum(gtf, axis=1, keepdims=True), axis=0, keepdims=True)
        e_incl, _, e_off = prefix(eqf)
        eq_rank = e_off + e_incl - eqf
        sel = gt | (eq & (eq_rank < need))
        self_ = jnp.where(sel, 1.0, 0.0)
        incl, tot, off = prefix(self_)
        ntab_ref[e] = tot[:, 0:128]
        otab_ref[e] = off[:, 0:128]
        incl_t = incl.T.astype(BF16)
        off_hi = off + tot

        def slots(c, carry2):
            s_row = lane + jnp.asarray(c * W, F32)
            oh = jnp.where((off <= s_row) & (s_row < off_hi), 1.0, 0.0)
            tile = jnp.sum(oh * tile_id, axis=0, keepdims=True)
            rank = s_row - jnp.sum(oh * off, axis=0, keepdims=True)
            rt = _dot(incl_t, oh.astype(BF16))
            pos = jnp.sum(jnp.where(rt <= rank, 1.0, 0.0), axis=0, keepdims=True)
            idx_ref[e, pl.ds(c, 1), :] = (tile * W + pos).astype(I32)
            return carry2

        lax.fori_loop(0, nsb, slots, 0)
        return carry

    lax.fori_loop(0, E, expert, 0)


def _select(aff3, cap):
    E, nt, W = aff3.shape
    return pl.pallas_call(
        functools.partial(_select_kernel, cap=cap),
        out_shape=[jax.ShapeDtypeStruct((E, cap // W, W), I32), jax.ShapeDtypeStruct((E, nt, 128), F32),
                   jax.ShapeDtypeStruct((E, nt, 128), F32)],
        scratch_shapes=[pltpu.VMEM((E, 1, 1), I32)],
        compiler_params=pltpu.CompilerParams(vmem_limit_bytes=VMEM_LIMIT),
    )(aff3)


SC_GATHER_ROWS = 128


def _sc_gather(table, idx):
    M = idx.shape[0]
    Dw = table.shape[1]
    win = SC_GATHER_ROWS
    info = plsc.get_sparse_core_info()
    nc = info.num_cores
    nw = nc * info.num_subcores
    per_w = M // nw
    assert M % (nw * win) == 0
    mesh = plsc.VectorSubcoreMesh(core_axis_name="c", subcore_axis_name="s")

    @functools.partial(
        pl.kernel, mesh=mesh,
        out_type=jax.ShapeDtypeStruct((M, Dw), table.dtype),
        scratch_types=[pltpu.VMEM((win,), I32), pltpu.VMEM((win, Dw), table.dtype), pltpu.SemaphoreType.DMA],
    )
    def gather(table_hbm, idx_hbm, out_hbm, idx_v, rows_v, sem):
        wid = lax.axis_index("s") * nc + lax.axis_index("c")

        @pl.loop(0, per_w // win)
        def _(j):
            base = pl.multiple_of(wid * per_w + j * win, win)
            pltpu.sync_copy(idx_hbm.at[pl.ds(base, win)], idx_v)
            pltpu.async_copy(table_hbm.at[idx_v], rows_v, sem).wait()
            pltpu.sync_copy(rows_v, out_hbm.at[pl.ds(base, win)])

    return gather(table, idx)


def _gate_up_kernel(wg_ref, wu_ref, o_ref):
    F = wg_ref.shape[2]
    o_ref[0, :, :F] = wg_ref[0].astype(BF16)
    o_ref[0, :, F:] = wu_ref[0].astype(BF16)


def _gate_up_weights(wg, wu):
    E, D, F = wg.shape
    rb = min(1024, D)
    src = pl.BlockSpec((1, rb, F), lambda e, r: (e, r, 0))
    return pl.pallas_call(
        _gate_up_kernel,
        grid=(E, D // rb),
        in_specs=[src, src],
        out_specs=pl.BlockSpec((1, rb, 2 * F), lambda e, r: (e, r, 0)),
        out_shape=jax.ShapeDtypeStruct((E, D, 2 * F), BF16),
        compiler_params=_cparams(("parallel", "parallel")),
    )(wg, wu)


def _ffn_kernel(xs_ref, wgu_ref, wd_ref, ye_ref):
    R = xs_ref.shape[0]
    half = wgu_ref.shape[1] // 2
    F = wd_ref.shape[1]
    xs = _unpack_rows(xs_ref[:, :half])
    side = xs_ref[:, half:]
    lane = lax.broadcasted_iota(I32, (R, SIDE_LANES), 1)
    aff = lax.bitcast_convert_type(side, F32)
    gate = jnp.sum(jnp.where(lane == pl.program_id(0), aff, 0.0), axis=1, keepdims=True)
    gu = _dot(xs, wgu_ref[0])
    g = gu[:, :F]
    hid = (g * jax.nn.sigmoid(g) * gu[:, F:]).astype(BF16)
    ye = _dot(hid, wd_ref[0]) * gate
    ye_ref[:, :half] = _pack_rows(ye.astype(BF16))
    ye_ref[:, half:] = side


def _expert_ffn(xs, wgu, wd):
    E, F, D = wd.shape
    cap = xs.shape[0] // E
    R = min(FFN_ROWS, cap)
    assert cap % R == 0 and wgu.shape == (E, D, 2 * F)
    RW = xs.shape[1]
    nsb = cap // R
    return pl.pallas_call(
        _ffn_kernel,
        grid=(E, nsb),
        in_specs=[pl.BlockSpec((R, RW), lambda e, s: (e * nsb + s, 0)),
                  pl.BlockSpec((1, D, 2 * F), lambda e, s: (e, 0, 0)),
                  pl.BlockSpec((1, F, D), lambda e, s: (e, 0, 0))],
        out_specs=pl.BlockSpec((R, RW), lambda e, s: (e * nsb + s, 0)),
        out_shape=jax.ShapeDtypeStruct(xs.shape, I32),
        compiler_params=_cparams(("parallel", "parallel")),
    )(xs, wgu, wd)


def _combine_tables(ntab, otab, cap):
    E, nt = ntab.shape
    W = COMBINE_ROWS
    g = COMBINE_TILE // ROUTE_TILE
    ntc = nt // g
    n2 = ntab.reshape(E, ntc, g).sum(-1)
    off2 = otab.reshape(E, ntc, g)[:, :, 0]
    hi = jnp.cumsum(n2, axis=0)
    lo = hi - n2
    m = hi[-1]
    nch = jnp.maximum(1, (m + W - 1) // W)
    cb_hi = jnp.cumsum(nch)
    cb_lo = cb_hi - nch
    nc_max = (E * cap) // W + ntc
    c = jnp.arange(nc_max, dtype=I32)
    tile = jnp.minimum(jnp.sum((cb_hi[None, :] <= c[:, None]).astype(I32), axis=1), ntc - 1)
    k = jnp.where(c < cb_hi[-1], c - cb_lo[tile], nch[ntc - 1] + c - cb_hi[-1])
    delta = jnp.arange(E, dtype=I32)[:, None] * cap + off2 - lo
    bcast = lambda a: jnp.broadcast_to(a.T.astype(F32)[:, :, None], (ntc, E, 128))
    blk = jnp.minimum(c, cb_hi[-1] - 1)
    sched = (tile, k.astype(I32), m.astype(I32), nch.astype(I32), blk.astype(I32))
    return sched, bcast(lo), bcast(hi), bcast(delta)


def _plan_kernel(tile_s, k_s, lo_ref, hi_ref, dl_ref, src_ref, *, n_rows):
    W = src_ref.shape[1]
    lane = lax.broadcasted_iota(I32, (1, W), 1).astype(F32)

    def chunk(c, carry):
        t = tile_s[c]
        q = lane + jnp.asarray(k_s[c] * W, F32)
        lo = lo_ref[t][:, 0:1]
        hi = hi_ref[t][:, 0:1]
        mine = (lo <= q) & (q < hi)
        src = jnp.sum(jnp.where(mine, dl_ref[t][:, 0:1] + q, 0.0), axis=0, keepdims=True)
        used = jnp.sum(jnp.where(mine, 1.0, 0.0), axis=0, keepdims=True) > 0.0
        p = lane + jnp.asarray(c * W, F32)
        spare = jnp.where(p < n_rows, p, p - n_rows)
        src_ref[pl.ds(c, 1), :] = jnp.where(used, src, spare).astype(I32)
        return carry

    lax.fori_loop(0, src_ref.shape[0], chunk, 0)


def _combine_plan(tile, k, lo, hi, delta, n_rows):
    smem = pl.BlockSpec(memory_space=pltpu.SMEM)
    vmem = pl.BlockSpec(memory_space=pltpu.VMEM)
    assert tile.shape[0] * COMBINE_ROWS <= 2 * n_rows
    return pl.pallas_call(
        functools.partial(_plan_kernel, n_rows=n_rows),
        in_specs=[smem, smem, vmem, vmem, vmem],
        out_specs=vmem,
        out_shape=jax.ShapeDtypeStruct((tile.shape[0], COMBINE_ROWS), I32),
        compiler_params=pltpu.CompilerParams(vmem_limit_bytes=VMEM_LIMIT),
    )(tile, k, lo, hi, delta)


def _combine_kernel(tile_s, k_s, m_s, nch_s, blk_s, x1_ref, rows_ref, g_ref, y_ref, acc_ref, *, E):
    c = pl.program_id(0)
    t = tile_s[c]
    k = k_s[c]
    W = rows_ref.shape[0]
    CT, D = x1_ref.shape

    @pl.when(k == 0)
    def _():
        acc_ref[...] = x1_ref[...]

    @pl.when(k < nch_s[t])
    def _():
        rows = _unpack_rows(rows_ref[:, :D // 2])
        side_t = rows_ref[:, D // 2:].astype(F32).T
        tok = side_t[E:E + 1, :]
        q = k * W + lax.broadcasted_iota(I32, (1, W), 1)
        hit = (tok == lax.broadcasted_iota(I32, (CT, W), 0).astype(F32)) & (q < m_s[t])
        onehot = jnp.where(hit, 1.0, 0.0).astype(BF16)
        acc_ref[...] += _dot(onehot, rows)

    @pl.when(k == nch_s[t] - 1)
    def _():
        y_ref[...] = _rms(acc_ref[...], g_ref[...])


def _combine(sched, x1, rows, g, E):
    T, D = x1.shape
    W = COMBINE_ROWS
    CT = COMBINE_TILE
    RW = rows.shape[1]
    grid_spec = pltpu.PrefetchScalarGridSpec(
        num_scalar_prefetch=len(sched),
        grid=(sched[0].shape[0],),
        in_specs=[pl.BlockSpec((CT, D), lambda c, tile_s, *_: (tile_s[c], 0)),
                  pl.BlockSpec((W, RW), lambda c, tile_s, k_s, m_s, nch_s, blk_s: (blk_s[c], 0)),
                  pl.BlockSpec((1, D), lambda c, *_: (0, 0))],
        out_specs=pl.BlockSpec((CT, D), lambda c, tile_s, *_: (tile_s[c], 0)),
        scratch_shapes=[pltpu.VMEM((CT, D), F32)],
    )
    return pl.pallas_call(
        functools.partial(_combine_kernel, E=E),
        grid_spec=grid_spec,
        out_shape=jax.ShapeDtypeStruct((T, D), F32),
        compiler_params=_cparams(("arbitrary",)),
    )(*sched, x1, rows, g)


def _layer(x, p, spec_cache):
    B, L, D = x.shape
    T = B * L
    da = p["conv_dw_w"].shape[1]
    db = p["hy_skip"].shape[1]
    E = p["n_experts"]
    W = ROUTE_TILE
    cap = max(1, min(T, CAPACITY_FACTOR * T // E))
    assert T % COMBINE_TILE == 0 and cap % W == 0 and L % CONV_ROWS == 0 and E < SIDE_LANES

    if L not in spec_cache:
        wf, wi = _dft_tables(L // 2)
        cw, sw = _twiddles(L, db)
        kspec = _filter_spectrum(L, p["hy_f_w1"], p["hy_f_b1"], p["hy_f_w2"], p["hy_f_b2"], p["hy_f_w3"],
                                 p["hy_f_b3"], p["hy_f_w4"], p["hy_f_freq"], wf, cw, sw)
        spec_cache[L] = (wf, wi, cw, sw, kspec)
    wf, wi, cw, sw, kspec = spec_cache[L]

    x2 = x.reshape(T, D)
    a, hy, gate = _inproj(x2, p["norm1_g"], p["w_in"], p["b_in"], da, db)
    act, x0, vv = _convs(a.reshape(B, L, da), hy.reshape(B, L, 3 * db), p["conv_dw_w"], p["conv_dw_b"],
                         p["conv_ln_g"], p["conv_ln_b"], p["hy_short_w"], p["hy_short_b"])
    u = _long_conv(vv, x0, wf, wi, cw, sw, kspec, p["hy_skip"])
    x1, tok, aff = _post(x2, act.reshape(T, da), u.reshape(T, db), gate, p["conv_w_out"], p["conv_b_out"],
                         p["hy_w_out"], p["hy_b_out"], p["w_o"], p["norm2_g"], p["router_w"], E)
    nt = T // W
    idx, ntab, otab = _select(aff.reshape(E, nt, W), cap)
    xs = _sc_gather(tok, idx.reshape(E * cap))
    ye = _expert_ffn(xs, p["exp_w_gate_up"], p["exp_w_down"])
    sched, lo, hi, delta = _combine_tables(ntab[:, :, 0].astype(I32), otab[:, :, 0].astype(I32), cap)
    src = _combine_plan(sched[0], sched[1], lo, hi, delta, E * cap)
    rows = _sc_gather(ye, src.reshape(-1))
    y = _combine(sched, x1, rows, p["norm_f_g"], E)
    return y.reshape(B, L, D)


def kernel(x_prompt, x_sample, norm1_g, w_in, b_in, conv_dw_w, conv_dw_b, conv_ln_g, conv_ln_b, conv_w_out, conv_b_out, hy_short_w, hy_short_b, hy_f_w1, hy_f_b1, hy_f_w2, hy_f_b2, hy_f_w3, hy_f_b3, hy_f_w4, hy_f_freq, hy_skip, hy_w_out, hy_b_out, w_o, norm2_g, router_w, exp_w_gate, exp_w_up, exp_w_down, norm_f_g):
    depth = w_in.shape[0]
    assert depth == 1
    i = 0
    row = lambda a: a[i][None].astype(F32)
    p = {
        "norm1_g": row(norm1_g), "w_in": w_in[i].astype(BF16), "b_in": row(b_in),
        "conv_dw_w": conv_dw_w[i].astype(F32), "conv_dw_b": row(conv_dw_b),
        "conv_ln_g": row(conv_ln_g), "conv_ln_b": row(conv_ln_b),
        "conv_w_out": conv_w_out[i].astype(BF16), "conv_b_out": row(conv_b_out),
        "hy_short_w": hy_short_w[i].astype(F32), "hy_short_b": row(hy_short_b),
        "hy_f_w1": hy_f_w1[i], "hy_f_b1": hy_f_b1[i], "hy_f_w2": hy_f_w2[i], "hy_f_b2": hy_f_b2[i],
        "hy_f_w3": hy_f_w3[i], "hy_f_b3": hy_f_b3[i], "hy_f_w4": hy_f_w4[i], "hy_f_freq": hy_f_freq[i],
        "hy_skip": row(hy_skip), "hy_w_out": hy_w_out[i].astype(BF16), "hy_b_out": row(hy_b_out),
        "w_o": w_o[i].astype(BF16), "norm2_g": row(norm2_g), "n_experts": router_w.shape[2],
        "router_w": jnp.pad(router_w[i], ((0, 0), (0, SIDE_LANES - router_w.shape[2]))).astype(BF16),
        "exp_w_gate_up": _gate_up_weights(exp_w_gate[i], exp_w_up[i]),
        "exp_w_down": exp_w_down[i].astype(BF16), "norm_f_g": norm_f_g[None].astype(F32),
    }
    spec_cache = {}
    y_sample = _layer(x_sample, p, spec_cache)
    y_prompt = _layer(x_prompt, p, spec_cache)
    return (y_prompt, y_sample)
```
